```python
import math
import jax, jax.numpy as jnp
from jax import lax
import numpy as np

D_MODEL = 1024
BATCH = 4
SEQ = 4096
DEPTH = 4

CHUNK = 64
N_MIXERS = 3
EPS = 1e-6

POOL_WINDOWS = (2, 4, 8, 16)
N_POOL_GROUPS = len(POOL_WINDOWS)
POOL_GROUP = D_MODEL // N_POOL_GROUPS

SB_HEAD_DIM = 64
SB_HEADS = D_MODEL // SB_HEAD_DIM
Q_BLOCK = 128

S5_GROUP = 16
S5_GROUPS = D_MODEL // S5_GROUP
S5_STATE = 64
S5_DT_MIN = 1e-3
S5_DT_MAX = 1e-1

MEM_LEN = 256
XA_HEADS = 4
XA_HEAD_DIM = D_MODEL // XA_HEADS

D_FF = ((8 * D_MODEL // 3 + 127) // 128) * 128
CONV_WIDTH = 3

N_POOL_LAYERS = (DEPTH + 2) // 3
N_SB_LAYERS = (DEPTH + 1) // 3
N_S5_LAYERS = DEPTH // 3

kernel_name = "hybrid_pool_stickbreak_s5_streaming_trunk"


def rms_norm(x, g):
    xf = x.astype(jnp.float32)
    y = xf * lax.rsqrt(jnp.mean(xf * xf, axis=-1, keepdims=True) + EPS)
    return (y * g.astype(jnp.float32)).astype(x.dtype)


def causal_shift(x, k):
    return jnp.pad(x, ((0, 0), (k, 0), (0, 0)))[:, : x.shape[1]]


def pool_mixer(h, w, scale):
    B, S, D = h.shape
    hf = h.astype(jnp.float32)
    cs = jnp.cumsum(hf, axis=1)
    counts = jnp.arange(1, S + 1, dtype=jnp.float32)
    groups = []
    for gi, win in enumerate(POOL_WINDOWS):
        sl = slice(gi * POOL_GROUP, (gi + 1) * POOL_GROUP)
        c = cs[..., sl]
        win_sum = c - causal_shift(c, win)
        cnt = jnp.minimum(counts, float(win))[None, :, None]
        groups.append(win_sum / cnt - hf[..., sl])
    p = jnp.stack(groups, axis=2).astype(h.dtype)
    y = jnp.einsum('bsgc,gcd->bsgd', p, w).reshape(B, S, D)
    return y * scale


def stick_breaking_attention(h, w_qkv, w_o):
    B, S, D = h.shape
    qkv = (h @ w_qkv).reshape(B, S, 3, SB_HEADS, SB_HEAD_DIM)
    q, k, v = qkv[:, :, 0], qkv[:, :, 1], qkv[:, :, 2]
    scale = SB_HEAD_DIM ** -0.5
    outs = []
    for i in range(S // Q_BLOCK):
        q0 = i * Q_BLOCK
        kend = q0 + Q_BLOCK
        z = jnp.einsum('bqhd,bkhd->bhqk', q[:, q0:kend], k[:, :kend]).astype(jnp.float32) * scale
        qpos = q0 + jnp.arange(Q_BLOCK)
        kpos = jnp.arange(kend)
        mask = kpos[None, :] < qpos[:, None]
        log_one_minus = jnp.where(mask, jax.nn.log_sigmoid(-z), 0.0)
        between = lax.cumsum(log_one_minus, axis=3, reverse=True) - log_one_minus
        a = jnp.where(mask, jnp.exp(jax.nn.log_sigmoid(z) + between), 0.0)
        outs.append(jnp.einsum('bhqk,bkhd->bqhd', a.astype(v.dtype), v[:, :kend]))
    o = jnp.concatenate(outs, axis=1).reshape(B, S, D)
    return o @ w_o


def _lin_rec_combine(left, right):
    a1, b1 = left
    a2, b2 = right
    return a1 * a2, a2 * b1 + b2


def s5_mixer(h, a_re, a_im, log_dt, b_re, b_im, c_re, c_im, d, w_glu):
    B, S, D = h.shape
    f32 = jnp.float32
    u = h.astype(f32)
    lam = lax.complex(a_re.astype(f32), a_im.astype(f32))
    dt = jnp.exp(log_dt.astype(f32))[:, None]
    dt_lam = lam * dt
    a_bar = jnp.exp(dt_lam)
    b_bar = ((a_bar - 1.0) / lam)[..., None] * lax.complex(b_re.astype(f32), b_im.astype(f32))
    c_mat = lax.complex(c_re.astype(f32), c_im.astype(f32))
    steps = jnp.arange(1, CHUNK + 1, dtype=f32)
    a_pow = jnp.exp(steps[:, None, None] * dt_lam[None])
    n_chunks = S // CHUNK
    u_chunks = u.reshape(B, n_chunks, CHUNK, S5_GROUPS, S5_GROUP).transpose(1, 0, 2, 3, 4)

    def chunk_step(state, u_blk):
        bu = jnp.einsum('bcgi,gpi->bcgp', u_blk.astype(jnp.complex64), b_bar)
        a_seq = jnp.broadcast_to(a_bar, bu.shape)
        _, xs = lax.associative_scan(_lin_rec_combine, (a_seq, bu), axis=1)
        xs = xs + a_pow[None] * state[:, None]
        y = jnp.real(jnp.einsum('bcgp,gip->bcgi', xs, c_mat))
        return xs[:, -1], y

    state0 = jnp.zeros((B, S5_GROUPS, S5_STATE), jnp.complex64)
    _, ys = lax.scan(chunk_step, state0, u_chunks)
    y = ys.transpose(1, 0, 2, 3, 4).reshape(B, S, D) + d.astype(f32) * u
    y = jax.nn.gelu(y).astype(h.dtype)
    val, gate = jnp.split(y @ w_glu, 2, axis=-1)
    return val * jax.nn.sigmoid(gate)


def memory_cross_attention(h, mem_n, wq, wkv, wo):
    B, S, D = h.shape
    M = mem_n.shape[1]
    q = (h @ wq).reshape(B, S, XA_HEADS, XA_HEAD_DIM)
    kv = (mem_n @ wkv).reshape(B, M, 2, XA_HEADS, XA_HEAD_DIM)
    k, v = kv[:, :, 0], kv[:, :, 1]
    s = jnp.einsum('bshd,bmhd->bhsm', q, k).astype(jnp.float32) * (XA_HEAD_DIM ** -0.5)
    p = jax.nn.softmax(s, axis=-1).astype(v.dtype)
    o = jnp.einsum('bhsm,bmhd->bshd', p, v).reshape(B, S, D)
    return o @ wo


def conv_glu_ffn(h, w_up, conv_w, conv_b, w_down):
    u = h @ w_up
    u = sum(conv_w[CONV_WIDTH - 1 - k] * causal_shift(u, k) for k in range(CONV_WIDTH)) + conv_b
    val, gate = jnp.split(u, 2, axis=-1)
    return (jax.nn.silu(gate) * val) @ w_down


def setup_inputs(seed: int = 0) -> dict:
    key = jax.random.key(seed)
    keys = iter(jax.random.split(key, 40))
    f32 = jnp.float32

    def nrm(shape, scale):
        return jax.random.normal(next(keys), shape, f32) * scale

    def gain(shape):
        return 1.0 + nrm(shape, 0.02)

    D, F = D_MODEL, D_FF
    G, P, Cg = S5_GROUPS, S5_STATE, S5_GROUP
    a_im_init = jnp.pi * jnp.arange(P, dtype=f32)
    return {
        "x": nrm((BATCH, SEQ, D), 1.0),
        "mem": nrm((BATCH, MEM_LEN, D), 1.0),
        "mix_norm_g": gain((DEPTH, D)),
        "pool_w": nrm((N_POOL_LAYERS, N_POOL_GROUPS, POOL_GROUP, POOL_GROUP), POOL_GROUP ** -0.5),
        "pool_scale": gain((N_POOL_LAYERS, D)),
        "sb_w_qkv": nrm((N_SB_LAYERS, D, 3 * D), D ** -0.5),
        "sb_w_o": nrm((N_SB_LAYERS, D, D), D ** -0.5),
        "s5_a_re": -0.5 + nrm((N_S5_LAYERS, G, P), 0.01),
        "s5_a_im": a_im_init + nrm((N_S5_LAYERS, G, P), 0.01),
        "s5_log_dt": jax.random.uniform(next(keys), (N_S5_LAYERS, G), f32,
                                        minval=math.log(S5_DT_MIN), maxval=math.log(S5_DT_MAX)),
        "s5_b_re": nrm((N_S5_LAYERS, G, P, Cg), (2 * Cg) ** -0.5),
        "s5_b_im": nrm((N_S5_LAYERS, G, P, Cg), (2 * Cg) ** -0.5),
        "s5_c_re": nrm((N_S5_LAYERS, G, Cg, P), (2 * P) ** -0.5),
        "s5_c_im": nrm((N_S5_LAYERS, G, Cg, P), (2 * P) ** -0.5),
        "s5_d": nrm((N_S5_LAYERS, D), 1.0),
        "s5_w_glu": nrm((N_S5_LAYERS, D, 2 * D), D ** -0.5),
        "xa_norm_g": gain((DEPTH, D)),
        "mem_norm_g": gain((DEPTH, D)),
        "xa_wq": nrm((DEPTH, D, D), D ** -0.5),
        "xa_wkv": nrm((DEPTH, D, 2 * D), D ** -0.5),
        "xa_wo": nrm((DEPTH, D, D), D ** -0.5),
        "ffn_norm_g": gain((DEPTH, D)),
        "ffn_w_up": nrm((DEPTH, D, 2 * F), D ** -0.5),
        "ffn_conv_w": nrm((DEPTH, CONV_WIDTH, 2 * F), CONV_WIDTH ** -0.5),
        "ffn_conv_b": nrm((DEPTH, 2 * F), 0.01),
        "ffn_w_down": nrm((DEPTH, F, D), F ** -0.5),
        "final_norm_g": gain((D,)),
    }


def reference(x, mem, mix_norm_g, pool_w, pool_scale, sb_w_qkv, sb_w_o,
              s5_a_re, s5_a_im, s5_log_dt, s5_b_re, s5_b_im, s5_c_re, s5_c_im, s5_d, s5_w_glu,
              xa_norm_g, mem_norm_g, xa_wq, xa_wkv, xa_wo,
              ffn_norm_g, ffn_w_up, ffn_conv_w, ffn_conv_b, ffn_w_down, final_norm_g):
    h = x
    for i in range(DEPTH):
        kind = i % N_MIXERS
        j = i // N_MIXERS
        hn = rms_norm(h, mix_norm_g[i])
        if kind == 0:
            t = pool_mixer(hn, pool_w[j], pool_scale[j])
        elif kind == 1:
            t = stick_breaking_attention(hn, sb_w_qkv[j], sb_w_o[j])
        else:
            t = s5_mixer(hn, s5_a_re[j], s5_a_im[j], s5_log_dt[j], s5_b_re[j], s5_b_im[j],
                         s5_c_re[j], s5_c_im[j], s5_d[j], s5_w_glu[j])
        h = h + t.astype(h.dtype)
        m = memory_cross_attention(rms_norm(h, xa_norm_g[i]), rms_norm(mem, mem_norm_g[i]),
                                   xa_wq[i], xa_wkv[i], xa_wo[i])
        h = h + m.astype(h.dtype)
        f = conv_glu_ffn(rms_norm(h, ffn_norm_g[i]), ffn_w_up[i], ffn_conv_w[i], ffn_conv_b[i], ffn_w_down[i])
        h = h + f.astype(h.dtype)
    return rms_norm(h, final_norm_g)
```

```python
import functools
import math

import jax
import jax.numpy as jnp
from jax import lax
from jax.experimental import pallas as pl
from jax.experimental.pallas import tpu as pltpu

F32 = jnp.float32
BF16 = jnp.bfloat16

D_MODEL = 1024
BATCH = 4
SEQ = 4096
TOKENS = BATCH * SEQ
DEPTH = 4
N_MIXERS = 3
EPS = 1e-6

POOL_WINDOWS = (2, 4, 8, 16)
POOL_GROUP = D_MODEL // len(POOL_WINDOWS)
POOL_HALO = 16

SB_HEAD_DIM = 64
SB_TQ = 256
SB_TK = 256

S5_GROUP = 16
S5_GROUPS = D_MODEL // S5_GROUP
S5_STATE = 64
S5_L = 16
S5_GB = 8
S5_NGB = S5_GROUPS // S5_GB
S5_Q = S5_GB * S5_STATE
S5_ROWS = TOKENS // S5_L
S5_RB = SEQ // S5_L

MEM_LEN = 256
XA_HEADS = 4
XA_HEAD_DIM = D_MODEL // XA_HEADS

D_FF = 2816
FFN_CHUNK = 256
FFN_HALO = 16
CONV_WIDTH = 3

ROW_TILE = 512
VMEM_LIMIT = 56 * 1024 * 1024


def _params(*sem):
    return pltpu.CompilerParams(dimension_semantics=sem, vmem_limit_bytes=VMEM_LIMIT)


def _rms(x, g):
    ms = jnp.mean(x * x, axis=-1, keepdims=True)
    return x * lax.rsqrt(ms + EPS) * g


def _dot(a, b):
    return jnp.dot(a, b, preferred_element_type=F32)


def _dot_nt(a, b, precision=None):
    return lax.dot_general(a, b, (((1,), (1,)), ((), ())),
                           preferred_element_type=F32, precision=precision)


def _shift_rows(x, k):
    return pltpu.roll(x, k, axis=0)


def _norm_kernel(x_ref, g_ref, o_ref):
    o_ref[...] = _rms(x_ref[...], g_ref[...]).astype(o_ref.dtype)


def _norm(x, g, out_dtype=F32):
    rows, d = x.shape
    return pl.pallas_call(
        _norm_kernel,
        grid=(rows // ROW_TILE,),
        in_specs=[pl.BlockSpec((ROW_TILE, d), lambda i: (i, 0)),
                  pl.BlockSpec((1, d), lambda i: (0, 0))],
        out_specs=pl.BlockSpec((ROW_TILE, d), lambda i: (i, 0)),
        out_shape=jax.ShapeDtypeStruct((rows, d), out_dtype),
        compiler_params=_params("parallel"),
        name="rmsnorm",
    )(x, g.reshape(1, d))


def _norm_matmul_kernel(x_ref, g_ref, w_ref, o_ref):
    hn = _rms(x_ref[...], g_ref[...]).astype(BF16)
    o_ref[...] = _dot(hn, w_ref[...]).astype(o_ref.dtype)


def _norm_matmul(x, g, w, name):
    rows, d = x.shape
    n = w.shape[1]
    return pl.pallas_call(
        _norm_matmul_kernel,
        grid=(rows // ROW_TILE,),
        in_specs=[pl.BlockSpec((ROW_TILE, d), lambda i: (i, 0)),
                  pl.BlockSpec((1, d), lambda i: (0, 0)),
                  pl.BlockSpec((d, n), lambda i: (0, 0))],
        out_specs=pl.BlockSpec((ROW_TILE, n), lambda i: (i, 0)),
        out_shape=jax.ShapeDtypeStruct((rows, n), BF16),
        compiler_params=_params("parallel"),
        name=name,
    )(x, g.reshape(1, d), w)


def _matmul_res_kernel(a_ref, w_ref, r_ref, o_ref):
    o_ref[...] = r_ref[...] + _dot(a_ref[...], w_ref[...])


def _matmul_res(a, w, res, name):
    rows, k = a.shape
    n = w.shape[1]
    return pl.pallas_call(
        _matmul_res_kernel,
        grid=(rows // ROW_TILE,),
        in_specs=[pl.BlockSpec((ROW_TILE, k), lambda i: (i, 0)),
                  pl.BlockSpec((k, n), lambda i: (0, 0)),
                  pl.BlockSpec((ROW_TILE, n), lambda i: (i, 0))],
        out_specs=pl.BlockSpec((ROW_TILE, n), lambda i: (i, 0)),
        out_shape=jax.ShapeDtypeStruct((rows, n), F32),
        compiler_params=_params("parallel"),
        name=name,
    )(a, w, res)


def _pool_kernel(h_ref, halo_ref, g_ref, w_ref, scale_ref, o_ref):
    i = pl.program_id(0)
    tiles_per_seq = SEQ // ROW_TILE
    x = h_ref[...]
    g = g_ref[...]
    hn = _rms(x, g)
    seq_start = (i % tiles_per_seq) == 0
    halo = jnp.where(seq_start, 0.0, _rms(halo_ref[...], g))
    ext = jnp.concatenate([halo, hn], axis=0)
    pos = (i % tiles_per_seq) * ROW_TILE + lax.broadcasted_iota(jnp.int32, (ROW_TILE, 1), 0)
    for gi, win in enumerate(POOL_WINDOWS):
        sl = slice(gi * POOL_GROUP, (gi + 1) * POOL_GROUP)
        s = ext[:, sl]
        k = 1
        while k < win:
            s = s + _shift_rows(s, k)
            k *= 2
        cnt = jnp.minimum(pos + 1, win).astype(F32)
        p = s[POOL_HALO:] / cnt - hn[:, sl]
        y = _dot(p.astype(BF16), w_ref[gi]) * scale_ref[:, sl]
        o_ref[:, sl] = x[:, sl] + y


def _pool_layer(h, g, w, scale):
    halo_blocks = ROW_TILE // POOL_HALO
    return pl.pallas_call(
        _pool_kernel,
        grid=(TOKENS // ROW_TILE,),
        in_specs=[pl.BlockSpec((ROW_TILE, D_MODEL), lambda i: (i, 0)),
                  pl.BlockSpec((POOL_HALO, D_MODEL),
                               lambda i: (jnp.maximum(i * halo_blocks - 1, 0), 0)),
                  pl.BlockSpec((1, D_MODEL), lambda i: (0, 0)),
                  pl.BlockSpec((len(POOL_WINDOWS), POOL_GROUP, POOL_GROUP), lambda i: (0, 0, 0)),
                  pl.BlockSpec((1, D_MODEL), lambda i: (0, 0))],
        out_specs=pl.BlockSpec((ROW_TILE, D_MODEL), lambda i: (i, 0)),
        out_shape=jax.ShapeDtypeStruct((TOKENS, D_MODEL), F32),
        compiler_params=_params("parallel"),
        name="pool_mixer",
    )(h, h, g.reshape(1, D_MODEL), w.astype(BF16), scale.reshape(1, D_MODEL))


def _sb_kernel(q_ref, k_ref, v_ref, o_ref):
    qi = pl.program_id(2)
    q = q_ref[...] * (SB_HEAD_DIM ** -0.5)
    lane = lax.broadcasted_iota(jnp.int32, (1, 2 * SB_HEAD_DIM), 1)
    row = lax.broadcasted_iota(jnp.int32, (SB_TQ, SB_TK), 0)
    col = lax.broadcasted_iota(jnp.int32, (SB_TQ, SB_TK), 1)
    later = (row > col).astype(BF16)
    causal = col < row

    def block(qh, j, carry, masked):
        run, acc = carry
        start = pl.multiple_of(j * SB_TK, SB_TK)
        kb = k_ref[pl.ds(start, SB_TK), :]
        vb = v_ref[pl.ds(start, SB_TK), :]
        z = _dot_nt(qh, kb)
        log_keep = -(jnp.maximum(z, 0.0) + jnp.log1p(jnp.exp(-jnp.abs(z))))
        if masked:
            log_keep = jnp.where(causal, log_keep, 0.0)
        hi = log_keep.astype(BF16)
        lo = (log_keep - hi.astype(F32)).astype(BF16)
        between = _dot(hi, later) + _dot(lo, later)
        a = jnp.exp(z + log_keep + between + run)
        if masked:
            a = jnp.where(causal, a, 0.0)
        acc = acc + _dot(a.astype(BF16), vb)
        run = run + jnp.sum(log_keep, axis=1, keepdims=True)
        return run, acc

    out = jnp.zeros((SB_TQ, 2 * SB_HEAD_DIM), F32)
    for hh in range(2):
        mine = (lane >= hh * SB_HEAD_DIM) & (lane < (hh + 1) * SB_HEAD_DIM)
        qh = jnp.where(mine, q, jnp.zeros_like(q))
        carry = (jnp.zeros((SB_TQ, 1), F32), jnp.zeros((SB_TQ, 2 * SB_HEAD_DIM), F32))
        carry = block(qh, qi, carry, True)
        carry = lax.fori_loop(
            0, qi, lambda n, c, qh=qh: block(qh, qi - 1 - n, c, False), carry)
        out = jnp.where(mine, carry[1], out)
    o_ref[...] = out.astype(o_ref.dtype)


def _sb_attention(qkv):
    pairs = D_MODEL // (2 * SB_HEAD_DIM)
    qblocks = SEQ // SB_TQ
    lanes = 2 * SB_HEAD_DIM
    return pl.pallas_call(
        _sb_kernel,
        grid=(BATCH, pairs, qblocks),
        in_specs=[pl.BlockSpec((SB_TQ, lanes), lambda b, p, i: (b * qblocks + i, p)),
                  pl.BlockSpec((SEQ, lanes), lambda b, p, i: (b, pairs + p)),
                  pl.BlockSpec((SEQ, lanes), lambda b, p, i: (b, 2 * pairs + p))],
        out_specs=pl.BlockSpec((SB_TQ, lanes), lambda b, p, i: (b * qblocks + i, p)),
        out_shape=jax.ShapeDtypeStruct((TOKENS, D_MODEL), BF16),
        compiler_params=_params("parallel", "parallel", "arbitrary"),
        name="sb_attention",
    )(qkv, qkv, qkv)


def _s5_param_kernel(lr_ref, li_ref, ldt_ref, btr_ref, bti_ref, ctr_ref, cti_ref,
                     bs_ref, cs_ref, dr_ref, a_ref):
    lr = lr_ref[...]
    li = li_ref[...]
    dt = jnp.exp(ldt_ref[...])
    mag = jnp.exp(dt * lr)
    ar = mag * jnp.cos(dt * li)
    ai = mag * jnp.sin(dt * li)
    den = lr * lr + li * li
    cfr = ((ar - 1.0) * lr + ai * li) / den
    cfi = (ai * lr - (ar - 1.0) * li) / den
    btr = btr_ref[...]
    bti = bti_ref[...]
    bbr = cfr * btr - cfi * bti
    bbi = cfr * bti + cfi * btr
    rows = S5_GB * S5_GROUP
    own = (lax.broadcasted_iota(jnp.int32, (rows, S5_Q), 0) // S5_GROUP
           == lax.broadcasted_iota(jnp.int32, (rows, S5_Q), 1) // S5_STATE)
    tile = lambda m: jnp.where(own, jnp.concatenate([m] * S5_GB, axis=0), 0.0)
    bbr, bbi = tile(bbr), tile(bbi)
    ccr, cci = tile(ctr_ref[...]), tile(cti_ref[...])
    pr = [jnp.ones_like(ar)]
    pi = [jnp.zeros_like(ai)]
    for _ in range(S5_L):
        pr.append(pr[-1] * ar - pi[-1] * ai)
        pi.append(pr[-2] * ai + pi[-1] * ar)
    b0 = jnp.concatenate([bbr, bbi], axis=1)
    for s in range(S5_L):
        k = S5_L - 1 - s
        blk = slice(s * rows, (s + 1) * rows)
        bs_ref[blk, :S5_Q] = (pr[k] * bbr - pi[k] * bbi).astype(BF16)
        bs_ref[blk, S5_Q:] = (pr[k] * bbi + pi[k] * bbr).astype(BF16)
    for k in range(S5_L + 1):
        zr = pr[k] * ccr - pi[k] * cci
        zi = pr[k] * cci + pi[k] * ccr
        z = jnp.concatenate([zr, -zi], axis=1)
        if k >= 1:
            cs_ref[(k - 1) * rows:k * rows, :] = z.astype(BF16)
        if k < S5_L:
            dk = _dot_nt(b0, z, precision=lax.Precision.HIGHEST)
            dr_ref[(S5_L - 1 - k) * rows:(S5_L - k) * rows, :] = dk.astype(BF16)
    a_ref[:, :S5_Q] = pr[S5_L]
    a_ref[:, S5_Q:] = pi[S5_L]


def _s5_operators(a_re, a_im, log_dt, b_re, b_im, c_re, c_im):
    lanes = lambda m: m.reshape(S5_NGB, 1, S5_Q)
    ldt = jnp.broadcast_to(log_dt[:, None], (S5_GROUPS, S5_STATE))
    bt = lambda m: m.reshape(S5_NGB, S5_GB, S5_STATE, S5_GROUP).transpose(0, 3, 1, 2).reshape(
        S5_NGB, S5_GROUP, S5_Q)
    ct = lambda m: m.reshape(S5_NGB, S5_GB, S5_GROUP, S5_STATE).transpose(0, 2, 1, 3).reshape(
        S5_NGB, S5_GROUP, S5_Q)
    rows = S5_L * S5_GB * S5_GROUP
    vec = pl.BlockSpec((None, 1, S5_Q), lambda i: (i, 0, 0))
    mat = pl.BlockSpec((None, S5_GROUP, S5_Q), lambda i: (i, 0, 0))
    return pl.pallas_call(
        _s5_param_kernel,
        grid=(S5_NGB,),
        in_specs=[vec, vec, vec, mat, mat, mat, mat],
        out_specs=[pl.BlockSpec((None, rows, 2 * S5_Q), lambda i: (i, 0, 0)),
                   pl.BlockSpec((None, rows, 2 * S5_Q), lambda i: (i, 0, 0)),
                   pl.BlockSpec((None, rows, S5_GB * S5_GROUP), lambda i: (i, 0, 0)),
                   pl.BlockSpec((None, 1, 2 * S5_Q), lambda i: (i, 0, 0))],
        out_shape=[jax.ShapeDtypeStruct((S5_NGB, rows, 2 * S5_Q), BF16),
                   jax.ShapeDtypeStruct((S5_NGB, rows, 2 * S5_Q), BF16),
                   jax.ShapeDtypeStruct((S5_NGB, rows, S5_GB * S5_GROUP), BF16),
                   jax.ShapeDtypeStruct((S5_NGB, 1, 2 * S5_Q), F32)],
        compiler_params=_params("parallel"),
        name="s5_operators",
    )(lanes(a_re), lanes(a_im), lanes(ldt), bt(b_re), bt(b_im), ct(c_re), ct(c_im))


def _gelu_tanh(x):
    c = math.sqrt(2.0 / math.pi)
    return 0.5 * x * (1.0 + jnp.tanh(c * (x + 0.044715 * (x * x * x))))


def _s5_kernel(*refs):
    u_refs = refs[:S5_L]
    bs_ref, cs_ref, dr_ref, a_ref, d_ref, o_ref = refs[S5_L:]
    lanes = S5_GB * S5_GROUP
    us = [r[...] for r in u_refs]
    ucat = jnp.concatenate([u.astype(BF16) for u in us], axis=1)
    v = _dot(ucat, bs_ref[...])
    xr, xi = v[:, :S5_Q], v[:, S5_Q:]
    cr, ci = a_ref[:, :S5_Q], a_ref[:, S5_Q:]
    row = lax.broadcasted_iota(jnp.int32, (S5_RB, 1), 0)
    k = 1
    while k < S5_RB:
        sr = jnp.where(row >= k, _shift_rows(xr, k), 0.0)
        si = jnp.where(row >= k, _shift_rows(xi, k), 0.0)
        xr, xi = xr + (cr * sr - ci * si), xi + (cr * si + ci * sr)
        cr, ci = cr * cr - ci * ci, 2.0 * (cr * ci)
        k *= 2
    prev = jnp.concatenate(
        [jnp.where(row >= 1, _shift_rows(xr, 1), 0.0),
         jnp.where(row >= 1, _shift_rows(xi, 1), 0.0)], axis=1).astype(BF16)
    d = d_ref[...]
    for t in range(S5_L):
        y = _dot(ucat[:, :(t + 1) * lanes], dr_ref[(S5_L - 1 - t) * lanes:, :])
        y = y + _dot_nt(prev, cs_ref[t * lanes:(t + 1) * lanes, :])
        y = y + d * us[t]
        o_ref[t] = _gelu_tanh(y).astype(o_ref.dtype)


def _s5_mixer_core(hn, ops, d):
    bs, cs, dr, a16 = ops
    u2d = hn.reshape(S5_ROWS, S5_L * D_MODEL)
    lanes = S5_GB * S5_GROUP
    rows = S5_L * lanes
    u_specs = [pl.BlockSpec((S5_RB, lanes), lambda g, b, s=s: (b, s * S5_NGB + g))
               for s in range(S5_L)]
    out = pl.pallas_call(
        _s5_kernel,
        grid=(S5_NGB, BATCH),
        in_specs=u_specs + [
            pl.BlockSpec((None, rows, 2 * S5_Q), lambda g, b: (g, 0, 0)),
            pl.BlockSpec((None, rows, 2 * S5_Q), lambda g, b: (g, 0, 0)),
            pl.BlockSpec((None, rows, lanes), lambda g, b: (g, 0, 0)),
            pl.BlockSpec((None, 1, 2 * S5_Q), lambda g, b: (g, 0, 0)),
            pl.BlockSpec((1, lanes), lambda g, b: (0, g))],
        out_specs=pl.BlockSpec((S5_L, S5_RB, lanes), lambda g, b: (0, b, g)),
        out_shape=jax.ShapeDtypeStruct((S5_L, S5_ROWS, D_MODEL), BF16),
        compiler_params=_params("parallel", "parallel"),
        name="s5_recurrence",
    )(*([u2d] * S5_L), bs, cs, dr, a16, d.reshape(1, D_MODEL))
    return out.transpose(1, 0, 2).reshape(TOKENS, D_MODEL)


def _glu_res_kernel(a_ref, w_ref, r_ref, o_ref):
    y = _dot(a_ref[...], w_ref[...])
    val, gate = y[:, :D_MODEL], y[:, D_MODEL:]
    o_ref[...] = r_ref[...] + val * (1.0 / (1.0 + jnp.exp(-gate)))


def _glu_res(a, w, res):
    return pl.pallas_call(
        _glu_res_kernel,
        grid=(TOKENS // ROW_TILE,),
        in_specs=[pl.BlockSpec((ROW_TILE, D_MODEL), lambda i: (i, 0)),
                  pl.BlockSpec((D_MODEL, 2 * D_MODEL), lambda i: (0, 0)),
                  pl.BlockSpec((ROW_TILE, D_MODEL), lambda i: (i, 0))],
        out_specs=pl.BlockSpec((ROW_TILE, D_MODEL), lambda i: (i, 0)),
        out_shape=jax.ShapeDtypeStruct((TOKENS, D_MODEL), F32),
        compiler_params=_params("parallel"),
        name="s5_glu",
    )(a, w, res)


def _xattn_kernel(h_ref, g_ref, wq_ref, k_ref, v_ref, wo_ref, o_ref):
    x = h_ref[...]
    hn = _rms(x, g_ref[...]).astype(BF16)
    q = (_dot(hn, wq_ref[...]) * (XA_HEAD_DIM ** -0.5)).astype(BF16)
    heads = []
    for hd in range(XA_HEADS):
        sl = slice(hd * XA_HEAD_DIM, (hd + 1) * XA_HEAD_DIM)
        s = _dot_nt(q[:, sl], k_ref[:, sl])
        e = jnp.exp(s - jnp.max(s, axis=-1, keepdims=True))
        denom = jnp.sum(e, axis=-1, keepdims=True)
        heads.append((_dot(e.astype(BF16), v_ref[:, sl]) / denom).astype(BF16))
    o = jnp.concatenate(heads, axis=1)
    o_ref[...] = x + _dot(o, wo_ref[...])


def _xattn_layer(h, g, wq, kv, wo):
    tiles_per_seq = SEQ // ROW_TILE
    return pl.pallas_call(
        _xattn_kernel,
        grid=(TOKENS // ROW_TILE,),
        in_specs=[pl.BlockSpec((ROW_TILE, D_MODEL), lambda i: (i, 0)),
                  pl.BlockSpec((1, D_MODEL), lambda i: (0, 0)),
                  pl.BlockSpec((D_MODEL, D_MODEL), lambda i: (0, 0)),
                  pl.BlockSpec((MEM_LEN, D_MODEL), lambda i: (i // tiles_per_seq, 0)),
                  pl.BlockSpec((MEM_LEN, D_MODEL), lambda i: (i // tiles_per_seq, 1)),
                  pl.BlockSpec((D_MODEL, D_MODEL), lambda i: (0, 0))],
        out_specs=pl.BlockSpec((ROW_TILE, D_MODEL), lambda i: (i, 0)),
        out_shape=jax.ShapeDtypeStruct((TOKENS, D_MODEL), F32),
        compiler_params=_params("parallel"),
        name="mem_xattn",
    )(h, g.reshape(1, D_MODEL), wq, kv, kv, wo)


def _ffn_kernel(h_ref, halo_ref, g_ref, wup_ref, cw_ref, cb_ref, wdn_ref, o_ref, acc_ref):
    i = pl.program_id(0)
    x = h_ref[...]
    g = g_ref[...]
    seq_start = (i % (SEQ // ROW_TILE)) == 0
    halo = jnp.where(seq_start, 0.0, _rms(halo_ref[...], g))
    hn = jnp.concatenate([halo, _rms(x, g)], axis=0).astype(BF16)

    def conv(u, cols):
        cw = cw_ref[:, cols]
        y = cw[2:3] * u + cw[1:2] * _shift_rows(u, 1) + cw[0:1] * _shift_rows(u, 2)
        return y[FFN_HALO:] + cb_ref[:, cols]

    for f in range(D_FF // FFN_CHUNK):
        vcols = slice(f * FFN_CHUNK, (f + 1) * FFN_CHUNK)
        gcols = slice(D_FF + f * FFN_CHUNK, D_FF + (f + 1) * FFN_CHUNK)
        val = conv(_dot(hn, wup_ref[:, vcols]), vcols)
        gate = conv(_dot(hn, wup_ref[:, gcols]), gcols)
        act = (gate * (1.0 / (1.0 + jnp.exp(-gate))) * val).astype(BF16)
        part = _dot(act, wdn_ref[vcols, :])
        if f == 0:
            acc_ref[...] = part
        else:
            acc_ref[...] += part
    o_ref[...] = x + acc_ref[...]


def _ffn_layer(h, g, w_up, conv_w, conv_b, w_down):
    halo_blocks = ROW_TILE // FFN_HALO
    resident = dict(pipeline_mode=pl.Buffered(1))
    return pl.pallas_call(
        _ffn_kernel,
        grid=(TOKENS // ROW_TILE,),
        in_specs=[pl.BlockSpec((ROW_TILE, D_MODEL), lambda i: (i, 0)),
                  pl.BlockSpec((FFN_HALO, D_MODEL),
                               lambda i: (jnp.maximum(i * halo_blocks - 1, 0), 0)),
                  pl.BlockSpec((1, D_MODEL), lambda i: (0, 0)),
                  pl.BlockSpec((D_MODEL, 2 * D_FF), lambda i: (0, 0), **resident),
                  pl.BlockSpec((CONV_WIDTH, 2 * D_FF), lambda i: (0, 0)),
                  pl.BlockSpec((1, 2 * D_FF), lambda i: (0, 0)),
                  pl.BlockSpec((D_FF, D_MODEL), lambda i: (0, 0), **resident)],
        out_specs=pl.BlockSpec((ROW_TILE, D_MODEL), lambda i: (i, 0)),
        out_shape=jax.ShapeDtypeStruct((TOKENS, D_MODEL), F32),
        scratch_shapes=[pltpu.VMEM((ROW_TILE, D_MODEL), F32)],
        compiler_params=_params("arbitrary"),
        name="conv_glu_ffn",
    )(h, h, g.reshape(1, D_MODEL), w_up, conv_w, conv_b.reshape(1, 2 * D_FF), w_down)


def kernel(x, mem, mix_norm_g, pool_w, pool_scale, sb_w_qkv, sb_w_o, s5_a_re, s5_a_im, s5_log_dt, s5_b_re, s5_b_im, s5_c_re, s5_c_im, s5_d, s5_w_glu, xa_norm_g, mem_norm_g, xa_wq, xa_wkv, xa_wo, ffn_norm_g, ffn_w_up, ffn_conv_w, ffn_conv_b, ffn_w_down, final_norm_g):
    h = x.reshape(TOKENS, D_MODEL)
    mem2d = mem.reshape(BATCH * MEM_LEN, D_MODEL)
    for i in range(DEPTH):
        kind = i % N_MIXERS
        j = i // N_MIXERS
        if kind == 0:
            h = _pool_layer(h, mix_norm_g[i], pool_w[j], pool_scale[j])
        elif kind == 1:
            qkv = _norm_matmul(h, mix_norm_g[i], sb_w_qkv[j].astype(BF16), "sb_qkv")
            o = _sb_attention(qkv)
            h = _matmul_res(o, sb_w_o[j].astype(BF16), h, "sb_out")
        else:
            ops = _s5_operators(s5_a_re[j], s5_a_im[j], s5_log_dt[j], s5_b_re[j], s5_b_im[j],
                                s5_c_re[j], s5_c_im[j])
            y = _s5_mixer_core(_norm(h, mix_norm_g[i]), ops, s5_d[j])
            h = _glu_res(y, s5_w_glu[j].astype(BF16), h)
        kv = _norm_matmul(mem2d, mem_norm_g[i], xa_wkv[i].astype(BF16), "mem_kv")
        h = _xattn_layer(h, xa_norm_g[i], xa_wq[i].astype(BF16), kv, xa_wo[i].astype(BF16))
        h = _ffn_layer(h, ffn_norm_g[i], ffn_w_up[i].astype(BF16), ffn_conv_w[i], ffn_conv_b[i],
                       ffn_w_down[i].astype(BF16))
    return _norm(h, final_norm_g).reshape(BATCH, SEQ, D_MODEL)
```

```python
import functools
import math

import jax
import jax.numpy as jnp
from jax import lax
from jax.experimental import pallas as pl
from jax.experimental.pallas import tpu as pltpu

F32 = jnp.float32
BF16 = jnp.bfloat16

D_MODEL = 1024
BATCH = 4
SEQ = 4096
TOKENS = BATCH * SEQ
DEPTH = 4
N_MIXERS = 3
EPS = 1e-6

POOL_WINDOWS = (2, 4, 8, 16)
POOL_GROUP = D_MODEL // len(POOL_WINDOWS)
POOL_HALO = 16

SB_HEAD_DIM = 64
SB_TQ = 256
SB_TK = 256
SB_DEAD = 110.0

S5_GROUP = 16
S5_GROUPS = D_MODEL // S5_GROUP
S5_STATE = 64
S5_L = 16
S5_GB = 8
S5_NGB = S5_GROUPS // S5_GB
S5_Q = S5_GB * S5_STATE
S5_ROWS = TOKENS // S5_L
S5_RB = SEQ // S5_L

MEM_LEN = 256
XA_HEADS = 4
XA_HEAD_DIM = D_MODEL // XA_HEADS

D_FF = 2816
FFN_CHUNK = 256
FFN_HALO = 16
CONV_WIDTH = 3

ROW_TILE = 512
VMEM_LIMIT = 56 * 1024 * 1024


def _params(*sem):
    return pltpu.CompilerParams(dimension_semantics=sem, vmem_limit_bytes=VMEM_LIMIT)


def _rms(x, g):
    ms = jnp.mean(x * x, axis=-1, keepdims=True)
    return x * lax.rsqrt(ms + EPS) * g


def _dot(a, b):
    return jnp.dot(a, b, preferred_element_type=F32)


def _dot_nt(a, b, precision=None):
    return lax.dot_general(a, b, (((1,), (1,)), ((), ())),
                           preferred_element_type=F32, precision=precision)


def _shift_rows(x, k):
    return pltpu.roll(x, k, axis=0)


def _norm_kernel(x_ref, g_ref, o_ref):
    o_ref[...] = _rms(x_ref[...], g_ref[...]).astype(o_ref.dtype)


def _norm(x, g, out_dtype=F32):
    rows, d = x.shape
    return pl.pallas_call(
        _norm_kernel,
        grid=(rows // ROW_TILE,),
        in_specs=[pl.BlockSpec((ROW_TILE, d), lambda i: (i, 0)),
                  pl.BlockSpec((1, d), lambda i: (0, 0))],
        out_specs=pl.BlockSpec((ROW_TILE, d), lambda i: (i, 0)),
        out_shape=jax.ShapeDtypeStruct((rows, d), out_dtype),
        compiler_params=_params("parallel"),
        name="rmsnorm",
    )(x, g.reshape(1, d))


def _norm_matmul_kernel(x_ref, g_ref, w_ref, o_ref):
    hn = _rms(x_ref[...], g_ref[...]).astype(BF16)
    o_ref[...] = _dot(hn, w_ref[...]).astype(o_ref.dtype)


def _norm_matmul(x, g, w, name):
    rows, d = x.shape
    n = w.shape[1]
    return pl.pallas_call(
        _norm_matmul_kernel,
        grid=(rows // ROW_TILE,),
        in_specs=[pl.BlockSpec((ROW_TILE, d), lambda i: (i, 0)),
                  pl.BlockSpec((1, d), lambda i: (0, 0)),
                  pl.BlockSpec((d, n), lambda i: (0, 0))],
        out_specs=pl.BlockSpec((ROW_TILE, n), lambda i: (i, 0)),
        out_shape=jax.ShapeDtypeStruct((rows, n), BF16),
        compiler_params=_params("parallel"),
        name=name,
    )(x, g.reshape(1, d), w)


def _matmul_res_kernel(a_ref, w_ref, r_ref, o_ref):
    o_ref[...] = r_ref[...] + _dot(a_ref[...], w_ref[...])


def _matmul_res(a, w, res, name):
    rows, k = a.shape
    n = w.shape[1]
    return pl.pallas_call(
        _matmul_res_kernel,
        grid=(rows // ROW_TILE,),
        in_specs=[pl.BlockSpec((ROW_TILE, k), lambda i: (i, 0)),
                  pl.BlockSpec((k, n), lambda i: (0, 0)),
                  pl.BlockSpec((ROW_TILE, n), lambda i: (i, 0))],
        out_specs=pl.BlockSpec((ROW_TILE, n), lambda i: (i, 0)),
        out_shape=jax.ShapeDtypeStruct((rows, n), F32),
        compiler_params=_params("parallel"),
        name=name,
    )(a, w, res)


def _pool_kernel(h_ref, halo_ref, g_ref, w_ref, scale_ref, o_ref):
    i = pl.program_id(0)
    tiles_per_seq = SEQ // ROW_TILE
    x = h_ref[...]
    g = g_ref[...]
    hn = _rms(x, g)
    seq_start = (i % tiles_per_seq) == 0
    halo = jnp.where(seq_start, 0.0, _rms(halo_ref[...], g))
    ext = jnp.concatenate([halo, hn], axis=0)
    pos = (i % tiles_per_seq) * ROW_TILE + lax.broadcasted_iota(jnp.int32, (ROW_TILE, 1), 0)
    for gi, win in enumerate(POOL_WINDOWS):
        sl = slice(gi * POOL_GROUP, (gi + 1) * POOL_GROUP)
        s = ext[:, sl]
        k = 1
        while k < win:
            s = s + _shift_rows(s, k)
            k *= 2
        cnt = jnp.minimum(pos + 1, win).astype(F32)
        p = s[POOL_HALO:] / cnt - hn[:, sl]
        y = _dot(p.astype(BF16), w_ref[gi]) * scale_ref[:, sl]
        o_ref[:, sl] = x[:, sl] + y


def _pool_layer(h, g, w, scale):
    halo_blocks = ROW_TILE // POOL_HALO
    return pl.pallas_call(
        _pool_kernel,
        grid=(TOKENS // ROW_TILE,),
        in_specs=[pl.BlockSpec((ROW_TILE, D_MODEL), lambda i: (i, 0)),
                  pl.BlockSpec((POOL_HALO, D_MODEL),
                               lambda i: (jnp.maximum(i * halo_blocks - 1, 0), 0)),
                  pl.BlockSpec((1, D_MODEL), lambda i: (0, 0)),
                  pl.BlockSpec((len(POOL_WINDOWS), POOL_GROUP, POOL_GROUP), lambda i: (0, 0, 0)),
                  pl.BlockSpec((1, D_MODEL), lambda i: (0, 0))],
        out_specs=pl.BlockSpec((ROW_TILE, D_MODEL), lambda i: (i, 0)),
        out_shape=jax.ShapeDtypeStruct((TOKENS, D_MODEL), F32),
        compiler_params=_params("parallel"),
        name="pool_mixer",
    )(h, h, g.reshape(1, D_MODEL), w.astype(BF16), scale.reshape(1, D_MODEL))


def _sb_kernel(q_ref, k_ref, v_ref, o_ref):
    qi = pl.program_id(2)
    q = q_ref[...] * (SB_HEAD_DIM ** -0.5)
    lane = lax.broadcasted_iota(jnp.int32, (1, 2 * SB_HEAD_DIM), 1)
    row = lax.broadcasted_iota(jnp.int32, (SB_TQ, SB_TK), 0)
    col = lax.broadcasted_iota(jnp.int32, (SB_TQ, SB_TK), 1)
    later = (row > col).astype(BF16)
    causal = col < row

    first = lane < SB_HEAD_DIM
    zero = jnp.zeros_like(q)
    qs = (jnp.where(first, q, zero), jnp.where(first, zero, q))

    def rows_of(ref, j):
        return ref[pl.ds(pl.multiple_of(j * SB_TK, SB_TK), SB_TK), :]

    def logits(j):
        kb = rows_of(k_ref, j)
        return tuple(_dot_nt(qh, kb) for qh in qs)

    def values(j):
        vb = rows_of(v_ref, j)
        vzero = jnp.zeros_like(vb)
        return jnp.concatenate([jnp.where(first, vb, vzero), jnp.where(first, vzero, vb)], axis=0)

    def weights(zs, runs, masked):
        nks = [jnp.maximum(z, 0.0) + jnp.log(1.0 + jnp.exp(-jnp.abs(z))) for z in zs]
        if masked:
            nks = [jnp.where(causal, nk, 0.0) for nk in nks]
        nbs = [_dot(nk.astype(BF16), later) for nk in nks]
        ws = [jnp.exp(z - nk - nb - run) for z, nk, nb, run in zip(zs, nks, nbs, runs)]
        if masked:
            ws = [jnp.where(causal, w, 0.0) for w in ws]
        wcat = jnp.concatenate([w.astype(BF16) for w in ws], axis=1)
        runs = tuple(run + jnp.sum(nk, axis=1, keepdims=True) for run, nk in zip(runs, nks))
        return wcat, runs

    def block(j, carry, masked):
        runs, acc = carry
        wcat, runs = weights(logits(j), runs, masked)
        return runs, acc + _dot(wcat, values(j))

    def step(state):
        n, _, runs, acc = state
        runs, acc = block(qi - 1 - n, (runs, acc), False)
        live = jnp.min(jnp.minimum(runs[0], runs[1])) <= SB_DEAD
        return n + 1, live, runs, acc

    carry = ((jnp.zeros((SB_TQ, 1), F32),) * 2, jnp.zeros((SB_TQ, 2 * SB_HEAD_DIM), F32))
    runs, acc = block(qi, carry, True)
    state = lax.while_loop(lambda s: (s[0] < qi) & s[1], step, (jnp.int32(0), True, runs, acc))
    o_ref[...] = state[3].astype(o_ref.dtype)


def _sb_attention(qkv):
    pairs = D_MODEL // (2 * SB_HEAD_DIM)
    qblocks = SEQ // SB_TQ
    lanes = 2 * SB_HEAD_DIM
    return pl.pallas_call(
        _sb_kernel,
        grid=(BATCH, pairs, qblocks),
        in_specs=[pl.BlockSpec((SB_TQ, lanes), lambda b, p, i: (b * qblocks + i, p)),
                  pl.BlockSpec((SEQ, lanes), lambda b, p, i: (b, pairs + p)),
                  pl.BlockSpec((SEQ, lanes), lambda b, p, i: (b, 2 * pairs + p))],
        out_specs=pl.BlockSpec((SB_TQ, lanes), lambda b, p, i: (b * qblocks + i, p)),
        out_shape=jax.ShapeDtypeStruct((TOKENS, D_MODEL), BF16),
        compiler_params=_params("parallel", "parallel", "arbitrary"),
        name="sb_attention",
    )(qkv, qkv, qkv)


def _s5_param_kernel(lr_ref, li_ref, ldt_ref, btr_ref, bti_ref, ctr_ref, cti_ref,
                     bs_ref, cs_ref, dr_ref, a_ref):
    lr = lr_ref[...]
    li = li_ref[...]
    dt = jnp.exp(ldt_ref[...])
    mag = jnp.exp(dt * lr)
    ar = mag * jnp.cos(dt * li)
    ai = mag * jnp.sin(dt * li)
    den = lr * lr + li * li
    cfr = ((ar - 1.0) * lr + ai * li) / den
    cfi = (ai * lr - (ar - 1.0) * li) / den
    btr = btr_ref[...]
    bti = bti_ref[...]
    bbr = cfr * btr - cfi * bti
    bbi = cfr * bti + cfi * btr
    rows = S5_GB * S5_GROUP
    own = (lax.broadcasted_iota(jnp.int32, (rows, S5_Q), 0) // S5_GROUP
           == lax.broadcasted_iota(jnp.int32, (rows, S5_Q), 1) // S5_STATE)
    tile = lambda m: jnp.where(own, jnp.concatenate([m] * S5_GB, axis=0), 0.0)
    bbr, bbi = tile(bbr), tile(bbi)
    ccr, cci = tile(ctr_ref[...]), tile(cti_ref[...])
    pr = [jnp.ones_like(ar)]
    pi = [jnp.zeros_like(ai)]
    for _ in range(S5_L):
        pr.append(pr[-1] * ar - pi[-1] * ai)
        pi.append(pr[-2] * ai + pi[-1] * ar)
    b0 = jnp.concatenate([bbr, bbi], axis=1)
    for s in range(S5_L):
        k = S5_L - 1 - s
        blk = slice(s * rows, (s + 1) * rows)
        bs_ref[blk, :S5_Q] = (pr[k] * bbr - pi[k] * bbi).astype(BF16)
        bs_ref[blk, S5_Q:] = (pr[k] * bbi + pi[k] * bbr).astype(BF16)
    for k in range(S5_L + 1):
        zr = pr[k] * ccr - pi[k] * cci
        zi = pr[k] * cci + pi[k] * ccr
        z = jnp.concatenate([zr, -zi], axis=1)
        if k >= 1:
            cs_ref[(k - 1) * rows:k * rows, :] = z.astype(BF16)
        if k < S5_L:
            dk = _dot_nt(b0, z, precision=lax.Precision.HIGHEST)
            dr_ref[(S5_L - 1 - k) * rows:(S5_L - k) * rows, :] = dk.astype(BF16)
    a_ref[:, :S5_Q] = pr[S5_L]
    a_ref[:, S5_Q:] = pi[S5_L]


def _s5_operators(a_re, a_im, log_dt, b_re, b_im, c_re, c_im):
    lanes = lambda m: m.reshape(S5_NGB, 1, S5_Q)
    ldt = jnp.broadcast_to(log_dt[:, None], (S5_GROUPS, S5_STATE))
    bt = lambda m: m.reshape(S5_NGB, S5_GB, S5_STATE, S5_GROUP).transpose(0, 3, 1, 2).reshape(
        S5_NGB, S5_GROUP, S5_Q)
    ct = lambda m: m.reshape(S5_NGB, S5_GB, S5_GROUP, S5_STATE).transpose(0, 2, 1, 3).reshape(
        S5_NGB, S5_GROUP, S5_Q)
    rows = S5_L * S5_GB * S5_GROUP
    vec = pl.BlockSpec((None, 1, S5_Q), lambda i: (i, 0, 0))
    mat = pl.BlockSpec((None, S5_GROUP, S5_Q), lambda i: (i, 0, 0))
    return pl.pallas_call(
        _s5_param_kernel,
        grid=(S5_NGB,),
        in_specs=[vec, vec, vec, mat, mat, mat, mat],
        out_specs=[pl.BlockSpec((None, rows, 2 * S5_Q), lambda i: (i, 0, 0)),
                   pl.BlockSpec((None, rows, 2 * S5_Q), lambda i: (i, 0, 0)),
                   pl.BlockSpec((None, rows, S5_GB * S5_GROUP), lambda i: (i, 0, 0)),
                   pl.BlockSpec((None, 1, 2 * S5_Q), lambda i: (i, 0, 0))],
        out_shape=[jax.ShapeDtypeStruct((S5_NGB, rows, 2 * S5_Q), BF16),
                   jax.ShapeDtypeStruct((S5_NGB, rows, 2 * S5_Q), BF16),
                   jax.ShapeDtypeStruct((S5_NGB, rows, S5_GB * S5_GROUP), BF16),
                   jax.ShapeDtypeStruct((S5_NGB, 1, 2 * S5_Q), F32)],
        compiler_params=_params("parallel"),
        name="s5_operators",
    )(lanes(a_re), lanes(a_im), lanes(ldt), bt(b_re), bt(b_im), ct(c_re), ct(c_im))


def _gelu_tanh(x):
    c = math.sqrt(2.0 / math.pi)
    return 0.5 * x * (1.0 + jnp.tanh(c * (x + 0.044715 * (x * x * x))))


def _s5_kernel(*refs):
    u_refs = refs[:S5_L]
    bs_ref, cs_ref, dr_ref, a_ref, d_ref, o_ref = refs[S5_L:]
    lanes = S5_GB * S5_GROUP
    us = [r[...] for r in u_refs]
    ucat = jnp.concatenate([u.astype(BF16) for u in us], axis=1)
    v = _dot(ucat, bs_ref[...])
    xr, xi = v[:, :S5_Q], v[:, S5_Q:]
    cr, ci = a_ref[:, :S5_Q], a_ref[:, S5_Q:]
    row = lax.broadcasted_iota(jnp.int32, (S5_RB, 1), 0)
    k = 1
    while k < S5_RB:
        sr = jnp.where(row >= k, _shift_rows(xr, k), 0.0)
        si = jnp.where(row >= k, _shift_rows(xi, k), 0.0)
        xr, xi = xr + (cr * sr - ci * si), xi + (cr * si + ci * sr)
        cr, ci = cr * cr - ci * ci, 2.0 * (cr * ci)
        k *= 2
    prev = jnp.concatenate(
        [jnp.where(row >= 1, _shift_rows(xr, 1), 0.0),
         jnp.where(row >= 1, _shift_rows(xi, 1), 0.0)], axis=1).astype(BF16)
    d = d_ref[...]
    for t in range(S5_L):
        y = _dot(ucat[:, :(t + 1) * lanes], dr_ref[(S5_L - 1 - t) * lanes:, :])
        y = y + _dot_nt(prev, cs_ref[t * lanes:(t + 1) * lanes, :])
        y = y + d * us[t]
        o_ref[t] = _gelu_tanh(y).astype(o_ref.dtype)


def _s5_mixer_core(hn, ops, d):
    bs, cs, dr, a16 = ops
    u2d = hn.reshape(S5_ROWS, S5_L * D_MODEL)
    lanes = S5_GB * S5_GROUP
    rows = S5_L * lanes
    u_specs = [pl.BlockSpec((S5_RB, lanes), lambda g, b, s=s: (b, s * S5_NGB + g))
               for s in range(S5_L)]
    out = pl.pallas_call(
        _s5_kernel,
        grid=(S5_NGB, BATCH),
        in_specs=u_specs + [
            pl.BlockSpec((None, rows, 2 * S5_Q), lambda g, b: (g, 0, 0)),
            pl.BlockSpec((None, rows, 2 * S5_Q), lambda g, b: (g, 0, 0)),
            pl.BlockSpec((None, rows, lanes), lambda g, b: (g, 0, 0)),
            pl.BlockSpec((None, 1, 2 * S5_Q), lambda g, b: (g, 0, 0)),
            pl.BlockSpec((1, lanes), lambda g, b: (0, g))],
        out_specs=pl.BlockSpec((S5_L, S5_RB, lanes), lambda g, b: (0, b, g)),
        out_shape=jax.ShapeDtypeStruct((S5_L, S5_ROWS, D_MODEL), BF16),
        compiler_params=_params("parallel", "parallel"),
        name="s5_recurrence",
    )(*([u2d] * S5_L), bs, cs, dr, a16, d.reshape(1, D_MODEL))
    return out.transpose(1, 0, 2).reshape(TOKENS, D_MODEL)


def _glu_res_kernel(a_ref, w_ref, r_ref, o_ref):
    y = _dot(a_ref[...], w_ref[...])
    val, gate = y[:, :D_MODEL], y[:, D_MODEL:]
    o_ref[...] = r_ref[...] + val * (1.0 / (1.0 + jnp.exp(-gate)))


def _glu_res(a, w, res):
    return pl.pallas_call(
        _glu_res_kernel,
        grid=(TOKENS // ROW_TILE,),
        in_specs=[pl.BlockSpec((ROW_TILE, D_MODEL), lambda i: (i, 0)),
                  pl.BlockSpec((D_MODEL, 2 * D_MODEL), lambda i: (0, 0)),
                  pl.BlockSpec((ROW_TILE, D_MODEL), lambda i: (i, 0))],
        out_specs=pl.BlockSpec((ROW_TILE, D_MODEL), lambda i: (i, 0)),
        out_shape=jax.ShapeDtypeStruct((TOKENS, D_MODEL), F32),
        compiler_params=_params("parallel"),
        name="s5_glu",
    )(a, w, res)


def _xattn_kernel(h_ref, g_ref, wq_ref, k_ref, v_ref, wo_ref, o_ref):
    x = h_ref[...]
    hn = _rms(x, g_ref[...]).astype(BF16)
    q = (_dot(hn, wq_ref[...]) * (XA_HEAD_DIM ** -0.5)).astype(BF16)
    heads = []
    for hd in range(XA_HEADS):
        sl = slice(hd * XA_HEAD_DIM, (hd + 1) * XA_HEAD_DIM)
        s = _dot_nt(q[:, sl], k_ref[:, sl])
        e = jnp.exp(s - jnp.max(s, axis=-1, keepdims=True))
        denom = jnp.sum(e, axis=-1, keepdims=True)
        heads.append((_dot(e.astype(BF16), v_ref[:, sl]) / denom).astype(BF16))
    o = jnp.concatenate(heads, axis=1)
    o_ref[...] = x + _dot(o, wo_ref[...])


def _xattn_layer(h, g, wq, kv, wo):
    tiles_per_seq = SEQ // ROW_TILE
    return pl.pallas_call(
        _xattn_kernel,
        grid=(TOKENS // ROW_TILE,),
        in_specs=[pl.BlockSpec((ROW_TILE, D_MODEL), lambda i: (i, 0)),
                  pl.BlockSpec((1, D_MODEL), lambda i: (0, 0)),
                  pl.BlockSpec((D_MODEL, D_MODEL), lambda i: (0, 0)),
                  pl.BlockSpec((MEM_LEN, D_MODEL), lambda i: (i // tiles_per_seq, 0)),
                  pl.BlockSpec((MEM_LEN, D_MODEL), lambda i: (i // tiles_per_seq, 1)),
                  pl.BlockSpec((D_MODEL, D_MODEL), lambda i: (0, 0))],
        out_specs=pl.BlockSpec((ROW_TILE, D_MODEL), lambda i: (i, 0)),
        out_shape=jax.ShapeDtypeStruct((TOKENS, D_MODEL), F32),
        compiler_params=_params("parallel"),
        name="mem_xattn",
    )(h, g.reshape(1, D_MODEL), wq, kv, kv, wo)


def _ffn_kernel(h_ref, halo_ref, g_ref, wup_ref, cw_ref, cb_ref, wdn_ref, o_ref, acc_ref):
    i = pl.program_id(0)
    x = h_ref[...]
    g = g_ref[...]
    seq_start = (i % (SEQ // ROW_TILE)) == 0
    halo = jnp.where(seq_start, 0.0, _rms(halo_ref[...], g))
    hn = jnp.concatenate([halo, _rms(x, g)], axis=0).astype(BF16)

    def conv(u, cols):
        cw = cw_ref[:, cols]
        y = cw[2:3] * u + cw[1:2] * _shift_rows(u, 1) + cw[0:1] * _shift_rows(u, 2)
        return y[FFN_HALO:] + cb_ref[:, cols]

    n_chunks = D_FF // FFN_CHUNK
    cols = lambda f, base: slice(base + f * FFN_CHUNK, base + (f + 1) * FFN_CHUNK)
    up = lambda f: (_dot(hn, wup_ref[:, cols(f, 0)]), _dot(hn, wup_ref[:, cols(f, D_FF)]))
    ahead = up(0)
    for f in range(n_chunks):
        vcols, gcols = cols(f, 0), cols(f, D_FF)
        u_val, u_gate = ahead
        if f + 1 < n_chunks:
            ahead = up(f + 1)
        val = conv(u_val, vcols)
        gate = conv(u_gate, gcols)
        act = (gate * (1.0 / (1.0 + jnp.exp(-gate))) * val).astype(BF16)
        part = _dot(act, wdn_ref[vcols, :])
        if f == 0:
            acc_ref[...] = part
        else:
            acc_ref[...] += part
    o_ref[...] = x + acc_ref[...]


def _ffn_layer(h, g, w_up, conv_w, conv_b, w_down):
    halo_blocks = ROW_TILE // FFN_HALO
    resident = dict(pipeline_mode=pl.Buffered(1))
    return pl.pallas_call(
        _ffn_kernel,
        grid=(TOKENS // ROW_TILE,),
        in_specs=[pl.BlockSpec((ROW_TILE, D_MODEL), lambda i: (i, 0)),
                  pl.BlockSpec((FFN_HALO, D_MODEL),
                               lambda i: (jnp.maximum(i * halo_blocks - 1, 0), 0)),
                  pl.BlockSpec((1, D_MODEL), lambda i: (0, 0)),
                  pl.BlockSpec((D_MODEL, 2 * D_FF), lambda i: (0, 0), **resident),
                  pl.BlockSpec((CONV_WIDTH, 2 * D_FF), lambda i: (0, 0)),
                  pl.BlockSpec((1, 2 * D_FF), lambda i: (0, 0)),
                  pl.BlockSpec((D_FF, D_MODEL), lambda i: (0, 0), **resident)],
        out_specs=pl.BlockSpec((ROW_TILE, D_MODEL), lambda i: (i, 0)),
        out_shape=jax.ShapeDtypeStruct((TOKENS, D_MODEL), F32),
        scratch_shapes=[pltpu.VMEM((ROW_TILE, D_MODEL), F32)],
        compiler_params=_params("arbitrary"),
        name="conv_glu_ffn",
    )(h, h, g.reshape(1, D_MODEL), w_up, conv_w, conv_b.reshape(1, 2 * D_FF), w_down)


def kernel(x, mem, mix_norm_g, pool_w, pool_scale, sb_w_qkv, sb_w_o, s5_a_re, s5_a_im, s5_log_dt, s5_b_re, s5_b_im, s5_c_re, s5_c_im, s5_d, s5_w_glu, xa_norm_g, mem_norm_g, xa_wq, xa_wkv, xa_wo, ffn_norm_g, ffn_w_up, ffn_conv_w, ffn_conv_b, ffn_w_down, final_norm_g):
    h = x.reshape(TOKENS, D_MODEL)
    mem2d = mem.reshape(BATCH * MEM_LEN, D_MODEL)
    for i in range(DEPTH):
        kind = i % N_MIXERS
        j = i // N_MIXERS
        if kind == 0:
            h = _pool_layer(h, mix_norm_g[i], pool_w[j], pool_scale[j])
        elif kind == 1:
            qkv = _norm_matmul(h, mix_norm_g[i], sb_w_qkv[j].astype(BF16), "sb_qkv")
            o = _sb_attention(qkv)
            h = _matmul_res(o, sb_w_o[j].astype(BF16), h, "sb_out")
        else:
            ops = _s5_operators(s5_a_re[j], s5_a_im[j], s5_log_dt[j], s5_b_re[j], s5_b_im[j],
                                s5_c_re[j], s5_c_im[j])
            y = _s5_mixer_core(_norm(h, mix_norm_g[i]), ops, s5_d[j])
            h = _glu_res(y, s5_w_glu[j].astype(BF16), h)
        kv = _norm_matmul(mem2d, mem_norm_g[i], xa_wkv[i].astype(BF16), "mem_kv")
        h = _xattn_layer(h, xa_norm_g[i], xa_wq[i].astype(BF16), kv, xa_wo[i].astype(BF16))
        h = _ffn_layer(h, ffn_norm_g[i], ffn_w_up[i].astype(BF16), ffn_conv_w[i], ffn_conv_b[i],
                       ffn_w_down[i].astype(BF16))
    return _norm(h, final_norm_g).reshape(BATCH, SEQ, D_MODEL)
```

```python
import functools
import math

import jax
import jax.numpy as jnp
from jax import lax
from jax.experimental import pallas as pl
from jax.experimental.pallas import tpu as pltpu

F32 = jnp.float32
BF16 = jnp.bfloat16

D_MODEL = 1024
BATCH = 4
SEQ = 4096
TOKENS = BATCH * SEQ
DEPTH = 4
N_MIXERS = 3
EPS = 1e-6

POOL_WINDOWS = (2, 4, 8, 16)
POOL_GROUP = D_MODEL // len(POOL_WINDOWS)
POOL_HALO = 16

SB_HEAD_DIM = 64
SB_TQ = 256
SB_TK = 256
SB_HEADS_PER_STEP = 4
SB_LANES = SB_HEADS_PER_STEP * SB_HEAD_DIM
SB_DEAD = 110.0

S5_GROUP = 16
S5_GROUPS = D_MODEL // S5_GROUP
S5_STATE = 64
S5_L = 16
S5_GB = 8
S5_NGB = S5_GROUPS // S5_GB
S5_Q = S5_GB * S5_STATE
S5_ROWS = TOKENS // S5_L
S5_RB = SEQ // S5_L

MEM_LEN = 256
XA_HEADS = 4
XA_HEAD_DIM = D_MODEL // XA_HEADS

D_FF = 2816
FFN_CHUNK = 256
FFN_RES = 8
FFN_HALO = 16
CONV_WIDTH = 3

ROW_TILE = 512
FFN_GROUPS = ROW_TILE // FFN_RES
VMEM_LIMIT = 56 * 1024 * 1024


def _params(*sem):
    return pltpu.CompilerParams(dimension_semantics=sem, vmem_limit_bytes=VMEM_LIMIT)


def _rms(x, g):
    ms = jnp.mean(x * x, axis=-1, keepdims=True)
    return x * lax.rsqrt(ms + EPS) * g


def _dot(a, b):
    return jnp.dot(a, b, preferred_element_type=F32)


def _dot_nt(a, b, precision=None):
    return lax.dot_general(a, b, (((1,), (1,)), ((), ())),
                           preferred_element_type=F32, precision=precision)


def _shift_rows(x, k):
    return pltpu.roll(x, k, axis=0)


def _norm_kernel(x_ref, g_ref, o_ref):
    o_ref[...] = _rms(x_ref[...], g_ref[...]).astype(o_ref.dtype)


def _norm(x, g, out_dtype=F32):
    rows, d = x.shape
    return pl.pallas_call(
        _norm_kernel,
        grid=(rows // ROW_TILE,),
        in_specs=[pl.BlockSpec((ROW_TILE, d), lambda i: (i, 0)),
                  pl.BlockSpec((1, d), lambda i: (0, 0))],
        out_specs=pl.BlockSpec((ROW_TILE, d), lambda i: (i, 0)),
        out_shape=jax.ShapeDtypeStruct((rows, d), out_dtype),
        compiler_params=_params("parallel"),
        name="rmsnorm",
    )(x, g.reshape(1, d))


def _norm_matmul_kernel(x_ref, g_ref, w_ref, o_ref):
    hn = _rms(x_ref[...], g_ref[...]).astype(BF16)
    o_ref[...] = _dot(hn, w_ref[...]).astype(o_ref.dtype)


def _norm_matmul(x, g, w, name):
    rows, d = x.shape
    n = w.shape[1]
    return pl.pallas_call(
        _norm_matmul_kernel,
        grid=(rows // ROW_TILE,),
        in_specs=[pl.BlockSpec((ROW_TILE, d), lambda i: (i, 0)),
                  pl.BlockSpec((1, d), lambda i: (0, 0)),
                  pl.BlockSpec((d, n), lambda i: (0, 0))],
        out_specs=pl.BlockSpec((ROW_TILE, n), lambda i: (i, 0)),
        out_shape=jax.ShapeDtypeStruct((rows, n), BF16),
        compiler_params=_params("parallel"),
        name=name,
    )(x, g.reshape(1, d), w)


def _matmul_res_kernel(a_ref, w_ref, r_ref, o_ref):
    o_ref[...] = r_ref[...] + _dot(a_ref[...], w_ref[...])


def _matmul_res(a, w, res, name):
    rows, k = a.shape
    n = w.shape[1]
    return pl.pallas_call(
        _matmul_res_kernel,
        grid=(rows // ROW_TILE,),
        in_specs=[pl.BlockSpec((ROW_TILE, k), lambda i: (i, 0)),
                  pl.BlockSpec((k, n), lambda i: (0, 0)),
                  pl.BlockSpec((ROW_TILE, n), lambda i: (i, 0))],
        out_specs=pl.BlockSpec((ROW_TILE, n), lambda i: (i, 0)),
        out_shape=jax.ShapeDtypeStruct((rows, n), F32),
        compiler_params=_params("parallel"),
        name=name,
    )(a, w, res)


def _pool_kernel(h_ref, halo_ref, g_ref, w_ref, scale_ref, o_ref):
    i = pl.program_id(0)
    tiles_per_seq = SEQ // ROW_TILE
    x = h_ref[...]
    g = g_ref[...]
    hn = _rms(x, g)
    seq_start = (i % tiles_per_seq) == 0
    halo = jnp.where(seq_start, 0.0, _rms(halo_ref[...], g))
    ext = jnp.concatenate([halo, hn], axis=0)
    pos = (i % tiles_per_seq) * ROW_TILE + lax.broadcasted_iota(jnp.int32, (ROW_TILE, 1), 0)
    for gi, win in enumerate(POOL_WINDOWS):
        sl = slice(gi * POOL_GROUP, (gi + 1) * POOL_GROUP)
        s = ext[:, sl]
        k = 1
        while k < win:
            s = s + _shift_rows(s, k)
            k *= 2
        cnt = jnp.minimum(pos + 1, win).astype(F32)
        p = s[POOL_HALO:] / cnt - hn[:, sl]
        y = _dot(p.astype(BF16), w_ref[gi]) * scale_ref[:, sl]
        o_ref[:, sl] = x[:, sl] + y


def _pool_layer(h, g, w, scale):
    halo_blocks = ROW_TILE // POOL_HALO
    return pl.pallas_call(
        _pool_kernel,
        grid=(TOKENS // ROW_TILE,),
        in_specs=[pl.BlockSpec((ROW_TILE, D_MODEL), lambda i: (i, 0)),
                  pl.BlockSpec((POOL_HALO, D_MODEL),
                               lambda i: (jnp.maximum(i * halo_blocks - 1, 0), 0)),
                  pl.BlockSpec((1, D_MODEL), lambda i: (0, 0)),
                  pl.BlockSpec((len(POOL_WINDOWS), POOL_GROUP, POOL_GROUP), lambda i: (0, 0, 0)),
                  pl.BlockSpec((1, D_MODEL), lambda i: (0, 0))],
        out_specs=pl.BlockSpec((ROW_TILE, D_MODEL), lambda i: (i, 0)),
        out_shape=jax.ShapeDtypeStruct((TOKENS, D_MODEL), F32),
        compiler_params=_params("parallel"),
        name="pool_mixer",
    )(h, h, g.reshape(1, D_MODEL), w.astype(BF16), scale.reshape(1, D_MODEL))


def _sb_kernel(q_ref, k_ref, v_ref, o_ref):
    qi = pl.program_id(2)
    q = q_ref[...] * (SB_HEAD_DIM ** -0.5)
    lane = lax.broadcasted_iota(jnp.int32, (1, SB_LANES), 1) // SB_HEAD_DIM
    row = lax.broadcasted_iota(jnp.int32, (SB_TQ, SB_TK), 0)
    col = lax.broadcasted_iota(jnp.int32, (SB_TQ, SB_TK), 1)
    later = (row > col).astype(BF16)
    causal = col < row

    heads = [lane == hh for hh in range(SB_HEADS_PER_STEP)]
    qs = [jnp.where(mine, q, jnp.zeros_like(q)) for mine in heads]

    def rows_of(ref, j):
        return ref[pl.ds(pl.multiple_of(j * SB_TK, SB_TK), SB_TK), :]

    def logits(j):
        kb = rows_of(k_ref, j)
        return [_dot_nt(qh, kb) for qh in qs]

    def values(j):
        vb = rows_of(v_ref, j)
        return jnp.concatenate([jnp.where(mine, vb, jnp.zeros_like(vb)) for mine in heads], axis=0)

    def weights(zs, runs, masked):
        nks = [jnp.maximum(z, 0.0) + jnp.log(1.0 + jnp.exp(-jnp.abs(z))) for z in zs]
        if masked:
            nks = [jnp.where(causal, nk, 0.0) for nk in nks]
        nbs = [_dot(nk.astype(BF16), later) for nk in nks]
        ws = [jnp.exp(z - nk - nb - run) for z, nk, nb, run in zip(zs, nks, nbs, runs)]
        if masked:
            ws = [jnp.where(causal, w, 0.0) for w in ws]
        wcat = jnp.concatenate([w.astype(BF16) for w in ws], axis=1)
        runs = tuple(run + jnp.sum(nk, axis=1, keepdims=True) for run, nk in zip(runs, nks))
        return wcat, runs

    def block(j, carry, masked):
        runs, acc = carry
        wcat, runs = weights(logits(j), runs, masked)
        return runs, acc + _dot(wcat, values(j))

    def step(state):
        n, _, runs, acc = state
        runs, acc = block(qi - 1 - n, (runs, acc), False)
        live = jnp.min(functools.reduce(jnp.minimum, runs)) <= SB_DEAD
        return n + 1, live, runs, acc

    prev = jnp.maximum(qi - 1, 0)
    runs = (jnp.zeros((SB_TQ, 1), F32),) * SB_HEADS_PER_STEP
    z_diag, z_prev = logits(qi), logits(prev)
    w_diag, runs = weights(z_diag, runs, True)
    w_prev, runs = weights(z_prev, runs, False)
    w_prev = jnp.where(qi > 0, w_prev, jnp.zeros_like(w_prev))
    acc = _dot(w_diag, values(qi)) + _dot(w_prev, values(prev))
    live = jnp.min(functools.reduce(jnp.minimum, runs)) <= SB_DEAD
    state = lax.while_loop(lambda s: (s[0] < qi) & s[1], step, (jnp.int32(1), live, runs, acc))
    o_ref[...] = state[3].astype(o_ref.dtype)


def _sb_attention(qkv):
    blocks = D_MODEL // SB_LANES
    qblocks = SEQ // SB_TQ
    return pl.pallas_call(
        _sb_kernel,
        grid=(BATCH, blocks, qblocks),
        in_specs=[pl.BlockSpec((SB_TQ, SB_LANES), lambda b, p, i: (b * qblocks + i, p)),
                  pl.BlockSpec((SEQ, SB_LANES), lambda b, p, i: (b, blocks + p)),
                  pl.BlockSpec((SEQ, SB_LANES), lambda b, p, i: (b, 2 * blocks + p))],
        out_specs=pl.BlockSpec((SB_TQ, SB_LANES), lambda b, p, i: (b * qblocks + i, p)),
        out_shape=jax.ShapeDtypeStruct((TOKENS, D_MODEL), BF16),
        compiler_params=_params("parallel", "parallel", "arbitrary"),
        name="sb_attention",
    )(qkv, qkv, qkv)


def _s5_param_kernel(lr_ref, li_ref, ldt_ref, btr_ref, bti_ref, ctr_ref, cti_ref,
                     bs_ref, cs_ref, dr_ref, a_ref):
    lr = lr_ref[...]
    li = li_ref[...]
    dt = jnp.exp(ldt_ref[...])
    mag = jnp.exp(dt * lr)
    ar = mag * jnp.cos(dt * li)
    ai = mag * jnp.sin(dt * li)
    den = lr * lr + li * li
    cfr = ((ar - 1.0) * lr + ai * li) / den
    cfi = (ai * lr - (ar - 1.0) * li) / den
    btr = btr_ref[...]
    bti = bti_ref[...]
    bbr = cfr * btr - cfi * bti
    bbi = cfr * bti + cfi * btr
    rows = S5_GB * S5_GROUP
    own = (lax.broadcasted_iota(jnp.int32, (rows, S5_Q), 0) // S5_GROUP
           == lax.broadcasted_iota(jnp.int32, (rows, S5_Q), 1) // S5_STATE)
    tile = lambda m: jnp.where(own, jnp.concatenate([m] * S5_GB, axis=0), 0.0)
    bbr, bbi = tile(bbr), tile(bbi)
    ccr, cci = tile(ctr_ref[...]), tile(cti_ref[...])
    pr = [jnp.ones_like(ar)]
    pi = [jnp.zeros_like(ai)]
    for _ in range(S5_L):
        pr.append(pr[-1] * ar - pi[-1] * ai)
        pi.append(pr[-2] * ai + pi[-1] * ar)
    b0 = jnp.concatenate([bbr, bbi], axis=1)
    for s in range(S5_L):
        k = S5_L - 1 - s
        blk = slice(s * rows, (s + 1) * rows)
        bs_ref[blk, :S5_Q] = (pr[k] * bbr - pi[k] * bbi).astype(BF16)
        bs_ref[blk, S5_Q:] = (pr[k] * bbi + pi[k] * bbr).astype(BF16)
    for k in range(S5_L + 1):
        zr = pr[k] * ccr - pi[k] * cci
        zi = pr[k] * cci + pi[k] * ccr
        z = jnp.concatenate([zr, -zi], axis=1)
        if k >= 1:
            cs_ref[(k - 1) * rows:k * rows, :] = z.astype(BF16)
        if k < S5_L:
            dk = _dot_nt(b0, z, precision=lax.Precision.HIGHEST).astype(BF16)
            dr_ref[(S5_L - 1 - k) * rows:(S5_L - k) * rows, rows:] = dk
            if k < S5_L - 1:
                dr_ref[(S5_L - 2 - k) * rows:(S5_L - 1 - k) * rows, :rows] = dk
    dr_ref[(S5_L - 1) * rows:, :rows] = jnp.zeros((rows, rows), BF16)
    a_ref[:, :S5_Q] = pr[S5_L]
    a_ref[:, S5_Q:] = pi[S5_L]


def _s5_operators(a_re, a_im, log_dt, b_re, b_im, c_re, c_im):
    lanes = lambda m: m.reshape(S5_NGB, 1, S5_Q)
    ldt = jnp.broadcast_to(log_dt[:, None], (S5_GROUPS, S5_STATE))
    bt = lambda m: m.reshape(S5_NGB, S5_GB, S5_STATE, S5_GROUP).transpose(0, 3, 1, 2).reshape(
        S5_NGB, S5_GROUP, S5_Q)
    ct = lambda m: m.reshape(S5_NGB, S5_GB, S5_GROUP, S5_STATE).transpose(0, 2, 1, 3).reshape(
        S5_NGB, S5_GROUP, S5_Q)
    rows = S5_L * S5_GB * S5_GROUP
    vec = pl.BlockSpec((None, 1, S5_Q), lambda i: (i, 0, 0))
    mat = pl.BlockSpec((None, S5_GROUP, S5_Q), lambda i: (i, 0, 0))
    return pl.pallas_call(
        _s5_param_kernel,
        grid=(S5_NGB,),
        in_specs=[vec, vec, vec, mat, mat, mat, mat],
        out_specs=[pl.BlockSpec((None, rows, 2 * S5_Q), lambda i: (i, 0, 0)),
                   pl.BlockSpec((None, rows, 2 * S5_Q), lambda i: (i, 0, 0)),
                   pl.BlockSpec((None, rows, 2 * S5_GB * S5_GROUP), lambda i: (i, 0, 0)),
                   pl.BlockSpec((None, 1, 2 * S5_Q), lambda i: (i, 0, 0))],
        out_shape=[jax.ShapeDtypeStruct((S5_NGB, rows, 2 * S5_Q), BF16),
                   jax.ShapeDtypeStruct((S5_NGB, rows, 2 * S5_Q), BF16),
                   jax.ShapeDtypeStruct((S5_NGB, rows, 2 * S5_GB * S5_GROUP), BF16),
                   jax.ShapeDtypeStruct((S5_NGB, 1, 2 * S5_Q), F32)],
        compiler_params=_params("parallel"),
        name="s5_operators",
    )(lanes(a_re), lanes(a_im), lanes(ldt), bt(b_re), bt(b_im), ct(c_re), ct(c_im))


def _gelu_tanh(x):
    c = math.sqrt(2.0 / math.pi)
    return 0.5 * x * (1.0 + jnp.tanh(c * (x + 0.044715 * (x * x * x))))


def _s5_kernel(*refs):
    u_refs = refs[:S5_L]
    bs_ref, cs_ref, dr_ref, a_ref, d_ref, o_ref = refs[S5_L:]
    lanes = S5_GB * S5_GROUP
    us = [r[...] for r in u_refs]
    ucat = jnp.concatenate([u.astype(BF16) for u in us], axis=1)
    v = _dot(ucat, bs_ref[...])
    xr, xi = v[:, :S5_Q], v[:, S5_Q:]
    cr, ci = a_ref[:, :S5_Q], a_ref[:, S5_Q:]
    row = lax.broadcasted_iota(jnp.int32, (S5_RB, 1), 0)
    k = 1
    while k < S5_RB:
        sr = jnp.where(row >= k, _shift_rows(xr, k), 0.0)
        si = jnp.where(row >= k, _shift_rows(xi, k), 0.0)
        xr, xi = xr + (cr * sr - ci * si), xi + (cr * si + ci * sr)
        cr, ci = cr * cr - ci * ci, 2.0 * (cr * ci)
        k *= 2
    prev = jnp.concatenate(
        [jnp.where(row >= 1, _shift_rows(xr, 1), 0.0),
         jnp.where(row >= 1, _shift_rows(xi, 1), 0.0)], axis=1).astype(BF16)
    carried = _dot_nt(prev, cs_ref[...])
    d = d_ref[...]
    for m in range(S5_L // 2):
        pair = _dot(ucat[:, :(2 * m + 2) * lanes], dr_ref[(S5_L - 2 - 2 * m) * lanes:, :])
        pair = pair + carried[:, 2 * m * lanes:(2 * m + 2) * lanes]
        for t in (2 * m, 2 * m + 1):
            y = pair[:, (t - 2 * m) * lanes:(t - 2 * m + 1) * lanes] + d * us[t]
            o_ref[t] = _gelu_tanh(y).astype(o_ref.dtype)


def _s5_mixer_core(hn, ops, d):
    bs, cs, dr, a16 = ops
    u2d = hn.reshape(S5_ROWS, S5_L * D_MODEL)
    lanes = S5_GB * S5_GROUP
    rows = S5_L * lanes
    u_specs = [pl.BlockSpec((S5_RB, lanes), lambda g, b, s=s: (b, s * S5_NGB + g))
               for s in range(S5_L)]
    out = pl.pallas_call(
        _s5_kernel,
        grid=(S5_NGB, BATCH),
        in_specs=u_specs + [
            pl.BlockSpec((None, rows, 2 * S5_Q), lambda g, b: (g, 0, 0)),
            pl.BlockSpec((None, rows, 2 * S5_Q), lambda g, b: (g, 0, 0)),
            pl.BlockSpec((None, rows, 2 * lanes), lambda g, b: (g, 0, 0)),
            pl.BlockSpec((None, 1, 2 * S5_Q), lambda g, b: (g, 0, 0)),
            pl.BlockSpec((1, lanes), lambda g, b: (0, g))],
        out_specs=pl.BlockSpec((S5_L, S5_RB, lanes), lambda g, b: (0, b, g)),
        out_shape=jax.ShapeDtypeStruct((S5_L, S5_ROWS, D_MODEL), BF16),
        compiler_params=_params("parallel", "parallel"),
        name="s5_recurrence",
    )(*([u2d] * S5_L), bs, cs, dr, a16, d.reshape(1, D_MODEL))
    return out


def _glu_res_kernel(a_ref, w_ref, r_ref, o_ref):
    y = _dot(a_ref[...], w_ref[...])
    val, gate = y[:, :D_MODEL], y[:, D_MODEL:]
    o_ref[...] = r_ref[...] + val * (1.0 / (1.0 + jnp.exp(-gate)))


def _glu_res(a, w, res):
    wide = S5_L * D_MODEL
    tiles = S5_ROWS // ROW_TILE
    out = pl.pallas_call(
        _glu_res_kernel,
        grid=(S5_L, tiles),
        in_specs=[pl.BlockSpec((None, ROW_TILE, D_MODEL), lambda t, i: (t, i, 0)),
                  pl.BlockSpec((D_MODEL, 2 * D_MODEL), lambda t, i: (0, 0)),
                  pl.BlockSpec((ROW_TILE, D_MODEL), lambda t, i: (i, t))],
        out_specs=pl.BlockSpec((ROW_TILE, D_MODEL), lambda t, i: (i, t)),
        out_shape=jax.ShapeDtypeStruct((S5_ROWS, wide), F32),
        compiler_params=_params("parallel", "parallel"),
        name="s5_glu",
    )(a, w, res.reshape(S5_ROWS, wide))
    return out.reshape(TOKENS, D_MODEL)


def _xattn_kernel(h_ref, g_ref, wq_ref, k_ref, v_ref, wo_ref, o_ref):
    x = h_ref[...]
    hn = _rms(x, g_ref[...]).astype(BF16)
    q = (_dot(hn, wq_ref[...]) * (XA_HEAD_DIM ** -0.5)).astype(BF16)
    heads = []
    for hd in range(XA_HEADS):
        sl = slice(hd * XA_HEAD_DIM, (hd + 1) * XA_HEAD_DIM)
        s = _dot_nt(q[:, sl], k_ref[:, sl])
        e = jnp.exp(s - jnp.max(s, axis=-1, keepdims=True))
        denom = jnp.sum(e, axis=-1, keepdims=True)
        heads.append((_dot(e.astype(BF16), v_ref[:, sl]) / denom).astype(BF16))
    o = jnp.concatenate(heads, axis=1)
    o_ref[...] = x + _dot(o, wo_ref[...])


def _xattn_layer(h, g, wq, kv, wo):
    tiles_per_seq = SEQ // ROW_TILE
    return pl.pallas_call(
        _xattn_kernel,
        grid=(TOKENS // ROW_TILE,),
        in_specs=[pl.BlockSpec((ROW_TILE, D_MODEL), lambda i: (i, 0)),
                  pl.BlockSpec((1, D_MODEL), lambda i: (0, 0)),
                  pl.BlockSpec((D_MODEL, D_MODEL), lambda i: (0, 0)),
                  pl.BlockSpec((MEM_LEN, D_MODEL), lambda i: (i // tiles_per_seq, 0)),
                  pl.BlockSpec((MEM_LEN, D_MODEL), lambda i: (i // tiles_per_seq, 1)),
                  pl.BlockSpec((D_MODEL, D_MODEL), lambda i: (0, 0))],
        out_specs=pl.BlockSpec((ROW_TILE, D_MODEL), lambda i: (i, 0)),
        out_shape=jax.ShapeDtypeStruct((TOKENS, D_MODEL), F32),
        compiler_params=_params("parallel"),
        name="mem_xattn",
    )(h, g.reshape(1, D_MODEL), wq, kv, kv, wo)


def _ffn_kernel(h_ref, halo_ref, g_ref, wup_ref, cw_ref, cb_ref, wdn_ref, o_ref, acc_ref):
    i = pl.program_id(0)
    g = g_ref[...]
    seq_start = (i % (SEQ // ROW_TILE)) == 0
    xs = [h_ref[:, s * D_MODEL:(s + 1) * D_MODEL] for s in range(FFN_RES)]
    xh = halo_ref[...]
    halo = jnp.concatenate([xh[:, :D_MODEL], xh[:, D_MODEL:]], axis=0)
    halo = jnp.where(seq_start, 0.0, _rms(halo, g))
    hn = jnp.concatenate([halo] + [_rms(x, g) for x in xs], axis=0).astype(BF16)
    first_group = lax.broadcasted_iota(jnp.int32, (FFN_GROUPS, 1), 0) == 0

    def conv(u, cols):
        cw = cw_ref[:, cols]
        cb = cb_ref[:, cols]
        blk = lambda s: u[FFN_HALO + s * FFN_GROUPS:FFN_HALO + (s + 1) * FFN_GROUPS]
        wrapped = lambda s, halo_row: jnp.where(first_group, u[halo_row:halo_row + 1],
                                                _shift_rows(blk(s), 1))
        b = [wrapped(FFN_RES - 2, FFN_HALO // 2 - 1), wrapped(FFN_RES - 1, FFN_HALO - 1)]
        b += [blk(s) for s in range(FFN_RES)]
        return jnp.concatenate(
            [cw[2:3] * b[s + 2] + cw[1:2] * b[s + 1] + cw[0:1] * b[s] + cb for s in range(FFN_RES)],
            axis=0)

    n_chunks = D_FF // FFN_CHUNK
    cols = lambda f, base: slice(base + f * FFN_CHUNK, base + (f + 1) * FFN_CHUNK)
    up = lambda f: (_dot(hn, wup_ref[:, cols(f, 0)]), _dot(hn, wup_ref[:, cols(f, D_FF)]))
    ahead = up(0)
    for f in range(n_chunks):
        vcols, gcols = cols(f, 0), cols(f, D_FF)
        u_val, u_gate = ahead
        if f + 1 < n_chunks:
            ahead = up(f + 1)
        val = conv(u_val, vcols)
        gate = conv(u_gate, gcols)
        act = (gate * (1.0 / (1.0 + jnp.exp(-gate))) * val).astype(BF16)
        part = _dot(act, wdn_ref[vcols, :])
        if f == 0:
            acc_ref[...] = part
        else:
            acc_ref[...] += part
    for s in range(FFN_RES):
        o_ref[:, s * D_MODEL:(s + 1) * D_MODEL] = xs[s] + acc_ref[s * FFN_GROUPS:(s + 1) * FFN_GROUPS, :]


def _ffn_layer(h, g, w_up, conv_w, conv_b, w_down):
    resident = dict(pipeline_mode=pl.Buffered(1))
    wide = FFN_RES * D_MODEL
    halo_blocks = FFN_GROUPS // (FFN_HALO // 2)
    out = pl.pallas_call(
        _ffn_kernel,
        grid=(TOKENS // ROW_TILE,),
        in_specs=[pl.BlockSpec((FFN_GROUPS, wide), lambda i: (i, 0)),
                  pl.BlockSpec((FFN_HALO // 2, 2 * D_MODEL),
                               lambda i: (jnp.maximum(i * halo_blocks - 1, 0), FFN_RES // 2 - 1)),
                  pl.BlockSpec((1, D_MODEL), lambda i: (0, 0)),
                  pl.BlockSpec((D_MODEL, 2 * D_FF), lambda i: (0, 0), **resident),
                  pl.BlockSpec((CONV_WIDTH, 2 * D_FF), lambda i: (0, 0)),
                  pl.BlockSpec((1, 2 * D_FF), lambda i: (0, 0)),
                  pl.BlockSpec((D_FF, D_MODEL), lambda i: (0, 0), **resident)],
        out_specs=pl.BlockSpec((FFN_GROUPS, wide), lambda i: (i, 0)),
        out_shape=jax.ShapeDtypeStruct((TOKENS // FFN_RES, wide), F32),
        scratch_shapes=[pltpu.VMEM((ROW_TILE, D_MODEL), F32)],
        compiler_params=_params("arbitrary"),
        name="conv_glu_ffn",
    )(h.reshape(TOKENS // FFN_RES, wide), h.reshape(TOKENS // FFN_RES, wide),
      g.reshape(1, D_MODEL), w_up, conv_w, conv_b.reshape(1, 2 * D_FF), w_down)
    return out.reshape(TOKENS, D_MODEL)


def kernel(x, mem, mix_norm_g, pool_w, pool_scale, sb_w_qkv, sb_w_o, s5_a_re, s5_a_im, s5_log_dt, s5_b_re, s5_b_im, s5_c_re, s5_c_im, s5_d, s5_w_glu, xa_norm_g, mem_norm_g, xa_wq, xa_wkv, xa_wo, ffn_norm_g, ffn_w_up, ffn_conv_w, ffn_conv_b, ffn_w_down, final_norm_g):
    h = x.reshape(TOKENS, D_MODEL)
    mem2d = mem.reshape(BATCH * MEM_LEN, D_MODEL)
    for i in range(DEPTH):
        kind = i % N_MIXERS
        j = i // N_MIXERS
        if kind == 0:
            h = _pool_layer(h, mix_norm_g[i], pool_w[j], pool_scale[j])
        elif kind == 1:
            qkv = _norm_matmul(h, mix_norm_g[i], sb_w_qkv[j].astype(BF16), "sb_qkv")
            o = _sb_attention(qkv)
            h = _matmul_res(o, sb_w_o[j].astype(BF16), h, "sb_out")
        else:
            ops = _s5_operators(s5_a_re[j], s5_a_im[j], s5_log_dt[j], s5_b_re[j], s5_b_im[j],
                                s5_c_re[j], s5_c_im[j])
            y = _s5_mixer_core(_norm(h, mix_norm_g[i]), ops, s5_d[j])
            h = _glu_res(y, s5_w_glu[j].astype(BF16), h)
        kv = _norm_matmul(mem2d, mem_norm_g[i], xa_wkv[i].astype(BF16), "mem_kv")
        h = _xattn_layer(h, xa_norm_g[i], xa_wq[i].astype(BF16), kv, xa_wo[i].astype(BF16))
        h = _ffn_layer(h, ffn_norm_g[i], ffn_w_up[i].astype(BF16), ffn_conv_w[i], ffn_conv_b[i],
                       ffn_w_down[i].astype(BF16))
    return _norm(h, final_norm_g).reshape(BATCH, SEQ, D_MODEL)
```

```python
import functools
import math

import jax
import jax.numpy as jnp
from jax import lax
from jax.experimental import pallas as pl
from jax.experimental.pallas import tpu as pltpu

F32 = jnp.float32
BF16 = jnp.bfloat16

D_MODEL = 1024
BATCH = 4
SEQ = 4096
TOKENS = BATCH * SEQ
DEPTH = 4
N_MIXERS = 3
EPS = 1e-6

POOL_WINDOWS = (2, 4, 8, 16)
POOL_GROUP = D_MODEL // len(POOL_WINDOWS)
POOL_HALO = 16

SB_HEAD_DIM = 64
SB_TQ = 256
SB_TK = 256
SB_HEADS_PER_STEP = 4
SB_LANES = SB_HEADS_PER_STEP * SB_HEAD_DIM
SB_DEAD = 110.0

S5_GROUP = 16
S5_GROUPS = D_MODEL // S5_GROUP
S5_STATE = 64
S5_L = 16
S5_GB = 8
S5_NGB = S5_GROUPS // S5_GB
S5_Q = S5_GB * S5_STATE
S5_ROWS = TOKENS // S5_L
S5_RB = SEQ // S5_L

MEM_LEN = 256
XA_HEADS = 4
XA_HEAD_DIM = D_MODEL // XA_HEADS

D_FF = 2816
FFN_CHUNK = 256
FFN_HALO = 16
CONV_WIDTH = 3

ROW_TILE = 512
VMEM_LIMIT = 56 * 1024 * 1024


def _params(*sem):
    return pltpu.CompilerParams(dimension_semantics=sem, vmem_limit_bytes=VMEM_LIMIT)


def _rms(x, g):
    ms = jnp.mean(x * x, axis=-1, keepdims=True)
    return x * lax.rsqrt(ms + EPS) * g


def _dot(a, b):
    return jnp.dot(a, b, preferred_element_type=F32)


def _dot_nt(a, b, precision=None):
    return lax.dot_general(a, b, (((1,), (1,)), ((), ())),
                           preferred_element_type=F32, precision=precision)


def _shift_rows(x, k):
    return pltpu.roll(x, k, axis=0)


def _norm_kernel(x_ref, g_ref, o_ref):
    o_ref[...] = _rms(x_ref[...], g_ref[...]).astype(o_ref.dtype)


def _norm(x, g, out_dtype=F32):
    rows, d = x.shape
    return pl.pallas_call(
        _norm_kernel,
        grid=(rows // ROW_TILE,),
        in_specs=[pl.BlockSpec((ROW_TILE, d), lambda i: (i, 0)),
                  pl.BlockSpec((1, d), lambda i: (0, 0))],
        out_specs=pl.BlockSpec((ROW_TILE, d), lambda i: (i, 0)),
        out_shape=jax.ShapeDtypeStruct((rows, d), out_dtype),
        compiler_params=_params("parallel"),
        name="rmsnorm",
    )(x, g.reshape(1, d))


def _norm_matmul_kernel(x_ref, g_ref, w_ref, o_ref):
    hn = _rms(x_ref[...], g_ref[...]).astype(BF16)
    o_ref[...] = _dot(hn, w_ref[...]).astype(o_ref.dtype)


def _norm_matmul(x, g, w, name):
    rows, d = x.shape
    n = w.shape[1]
    return pl.pallas_call(
        _norm_matmul_kernel,
        grid=(rows // ROW_TILE,),
        in_specs=[pl.BlockSpec((ROW_TILE, d), lambda i: (i, 0)),
                  pl.BlockSpec((1, d), lambda i: (0, 0)),
                  pl.BlockSpec((d, n), lambda i: (0, 0))],
        out_specs=pl.BlockSpec((ROW_TILE, n), lambda i: (i, 0)),
        out_shape=jax.ShapeDtypeStruct((rows, n), BF16),
        compiler_params=_params("parallel"),
        name=name,
    )(x, g.reshape(1, d), w)


def _matmul_res_kernel(a_ref, w_ref, r_ref, o_ref):
    o_ref[...] = r_ref[...] + _dot(a_ref[...], w_ref[...])


def _matmul_res(a, w, res, name):
    rows, k = a.shape
    n = w.shape[1]
    return pl.pallas_call(
        _matmul_res_kernel,
        grid=(rows // ROW_TILE,),
        in_specs=[pl.BlockSpec((ROW_TILE, k), lambda i: (i, 0)),
                  pl.BlockSpec((k, n), lambda i: (0, 0)),
                  pl.BlockSpec((ROW_TILE, n), lambda i: (i, 0))],
        out_specs=pl.BlockSpec((ROW_TILE, n), lambda i: (i, 0)),
        out_shape=jax.ShapeDtypeStruct((rows, n), F32),
        compiler_params=_params("parallel"),
        name=name,
    )(a, w, res)


def _pool_kernel(h_ref, halo_ref, g_ref, w_ref, scale_ref, o_ref):
    i = pl.program_id(0)
    tiles_per_seq = SEQ // ROW_TILE
    x = h_ref[...]
    g = g_ref[...]
    hn = _rms(x, g)
    seq_start = (i % tiles_per_seq) == 0
    halo = jnp.where(seq_start, 0.0, _rms(halo_ref[...], g))
    ext = jnp.concatenate([halo, hn], axis=0)
    pos = (i % tiles_per_seq) * ROW_TILE + lax.broadcasted_iota(jnp.int32, (ROW_TILE, 1), 0)
    for gi, win in enumerate(POOL_WINDOWS):
        sl = slice(gi * POOL_GROUP, (gi + 1) * POOL_GROUP)
        s = ext[:, sl]
        k = 1
        while k < win:
            s = s + _shift_rows(s, k)
            k *= 2
        cnt = jnp.minimum(pos + 1, win).astype(F32)
        p = s[POOL_HALO:] / cnt - hn[:, sl]
        y = _dot(p.astype(BF16), w_ref[gi]) * scale_ref[:, sl]
        o_ref[:, sl] = x[:, sl] + y


def _pool_layer(h, g, w, scale):
    halo_blocks = ROW_TILE // POOL_HALO
    return pl.pallas_call(
        _pool_kernel,
        grid=(TOKENS // ROW_TILE,),
        in_specs=[pl.BlockSpec((ROW_TILE, D_MODEL), lambda i: (i, 0)),
                  pl.BlockSpec((POOL_HALO, D_MODEL),
                               lambda i: (jnp.maximum(i * halo_blocks - 1, 0), 0)),
                  pl.BlockSpec((1, D_MODEL), lambda i: (0, 0)),
                  pl.BlockSpec((len(POOL_WINDOWS), POOL_GROUP, POOL_GROUP), lambda i: (0, 0, 0)),
                  pl.BlockSpec((1, D_MODEL), lambda i: (0, 0))],
        out_specs=pl.BlockSpec((ROW_TILE, D_MODEL), lambda i: (i, 0)),
        out_shape=jax.ShapeDtypeStruct((TOKENS, D_MODEL), F32),
        compiler_params=_params("parallel"),
        name="pool_mixer",
    )(h, h, g.reshape(1, D_MODEL), w.astype(BF16), scale.reshape(1, D_MODEL))


def _sb_kernel(q_ref, k_ref, v_ref, o_ref):
    qi = pl.program_id(2)
    q = q_ref[...] * (SB_HEAD_DIM ** -0.5)
    lane = lax.broadcasted_iota(jnp.int32, (1, SB_LANES), 1) // SB_HEAD_DIM
    row = lax.broadcasted_iota(jnp.int32, (SB_TQ, SB_TK), 0)
    col = lax.broadcasted_iota(jnp.int32, (SB_TQ, SB_TK), 1)
    later = (row > col).astype(BF16)
    causal = col < row

    heads = [lane == hh for hh in range(SB_HEADS_PER_STEP)]
    qs = [jnp.where(mine, q, jnp.zeros_like(q)) for mine in heads]

    def rows_of(ref, j):
        return ref[pl.ds(pl.multiple_of(j * SB_TK, SB_TK), SB_TK), :]

    def logits(j):
        kb = rows_of(k_ref, j)
        return [_dot_nt(qh, kb) for qh in qs]

    def values(j):
        vb = rows_of(v_ref, j)
        return jnp.concatenate([jnp.where(mine, vb, jnp.zeros_like(vb)) for mine in heads], axis=0)

    def weights(zs, runs, masked):
        nks = [jnp.maximum(z, 0.0) + jnp.log(1.0 + jnp.exp(-jnp.abs(z))) for z in zs]
        if masked:
            nks = [jnp.where(causal, nk, 0.0) for nk in nks]
        nbs = [_dot(nk.astype(BF16), later) for nk in nks]
        ws = [jnp.exp(z - nk - nb - run) for z, nk, nb, run in zip(zs, nks, nbs, runs)]
        if masked:
            ws = [jnp.where(causal, w, 0.0) for w in ws]
        wcat = jnp.concatenate([w.astype(BF16) for w in ws], axis=1)
        runs = tuple(run + jnp.sum(nk, axis=1, keepdims=True) for run, nk in zip(runs, nks))
        return wcat, runs

    def block(j, carry, masked):
        runs, acc = carry
        wcat, runs = weights(logits(j), runs, masked)
        return runs, acc + _dot(wcat, values(j))

    def step(state):
        n, _, runs, acc = state
        runs, acc = block(qi - 1 - n, (runs, acc), False)
        live = jnp.min(functools.reduce(jnp.minimum, runs)) <= SB_DEAD
        return n + 1, live, runs, acc

    prev = jnp.maximum(qi - 1, 0)
    runs = (jnp.zeros((SB_TQ, 1), F32),) * SB_HEADS_PER_STEP
    z_diag, z_prev = logits(qi), logits(prev)
    w_diag, runs = weights(z_diag, runs, True)
    w_prev, runs = weights(z_prev, runs, False)
    w_prev = jnp.where(qi > 0, w_prev, jnp.zeros_like(w_prev))
    acc = _dot(w_diag, values(qi)) + _dot(w_prev, values(prev))
    live = jnp.min(functools.reduce(jnp.minimum, runs)) <= SB_DEAD
    state = lax.while_loop(lambda s: (s[0] < qi) & s[1], step, (jnp.int32(1), live, runs, acc))
    o_ref[...] = state[3].astype(o_ref.dtype)


def _sb_attention(qkv):
    blocks = D_MODEL // SB_LANES
    qblocks = SEQ // SB_TQ
    return pl.pallas_call(
        _sb_kernel,
        grid=(BATCH, blocks, qblocks),
        in_specs=[pl.BlockSpec((SB_TQ, SB_LANES), lambda b, p, i: (b * qblocks + i, p)),
                  pl.BlockSpec((SEQ, SB_LANES), lambda b, p, i: (b, blocks + p)),
                  pl.BlockSpec((SEQ, SB_LANES), lambda b, p, i: (b, 2 * blocks + p))],
        out_specs=pl.BlockSpec((SB_TQ, SB_LANES), lambda b, p, i: (b * qblocks + i, p)),
        out_shape=jax.ShapeDtypeStruct((TOKENS, D_MODEL), BF16),
        compiler_params=_params("parallel", "parallel", "arbitrary"),
        name="sb_attention",
    )(qkv, qkv, qkv)


def _s5_param_kernel(lr_ref, li_ref, ldt_ref, btr_ref, bti_ref, ctr_ref, cti_ref,
                     bs_ref, cs_ref, dr_ref, a_ref):
    lr = lr_ref[...]
    li = li_ref[...]
    dt = jnp.exp(ldt_ref[...])
    mag = jnp.exp(dt * lr)
    ar = mag * jnp.cos(dt * li)
    ai = mag * jnp.sin(dt * li)
    den = lr * lr + li * li
    cfr = ((ar - 1.0) * lr + ai * li) / den
    cfi = (ai * lr - (ar - 1.0) * li) / den
    btr = btr_ref[...]
    bti = bti_ref[...]
    bbr = cfr * btr - cfi * bti
    bbi = cfr * bti + cfi * btr
    rows = S5_GB * S5_GROUP
    own = (lax.broadcasted_iota(jnp.int32, (rows, S5_Q), 0) // S5_GROUP
           == lax.broadcasted_iota(jnp.int32, (rows, S5_Q), 1) // S5_STATE)
    tile = lambda m: jnp.where(own, jnp.concatenate([m] * S5_GB, axis=0), 0.0)
    bbr, bbi = tile(bbr), tile(bbi)
    ccr, cci = tile(ctr_ref[...]), tile(cti_ref[...])
    pr = [jnp.ones_like(ar)]
    pi = [jnp.zeros_like(ai)]
    for _ in range(S5_L):
        pr.append(pr[-1] * ar - pi[-1] * ai)
        pi.append(pr[-2] * ai + pi[-1] * ar)
    b0 = jnp.concatenate([bbr, bbi], axis=1)
    for s in range(S5_L):
        k = S5_L - 1 - s
        blk = slice(s * rows, (s + 1) * rows)
        bs_ref[blk, :S5_Q] = (pr[k] * bbr - pi[k] * bbi).astype(BF16)
        bs_ref[blk, S5_Q:] = (pr[k] * bbi + pi[k] * bbr).astype(BF16)
    for k in range(S5_L + 1):
        zr = pr[k] * ccr - pi[k] * cci
        zi = pr[k] * cci + pi[k] * ccr
        z = jnp.concatenate([zr, -zi], axis=1)
        if k >= 1:
            cs_ref[(k - 1) * rows:k * rows, :] = z.astype(BF16)
        if k < S5_L:
            dk = _dot_nt(b0, z, precision=lax.Precision.HIGHEST).astype(BF16)
            dr_ref[(S5_L - 1 - k) * rows:(S5_L - k) * rows, rows:] = dk
            if k < S5_L - 1:
                dr_ref[(S5_L - 2 - k) * rows:(S5_L - 1 - k) * rows, :rows] = dk
    dr_ref[(S5_L - 1) * rows:, :rows] = jnp.zeros((rows, rows), BF16)
    a_ref[:, :S5_Q] = pr[S5_L]
    a_ref[:, S5_Q:] = pi[S5_L]


def _s5_operators(a_re, a_im, log_dt, b_re, b_im, c_re, c_im):
    lanes = lambda m: m.reshape(S5_NGB, 1, S5_Q)
    ldt = jnp.broadcast_to(log_dt[:, None], (S5_GROUPS, S5_STATE))
    bt = lambda m: m.reshape(S5_NGB, S5_GB, S5_STATE, S5_GROUP).transpose(0, 3, 1, 2).reshape(
        S5_NGB, S5_GROUP, S5_Q)
    ct = lambda m: m.reshape(S5_NGB, S5_GB, S5_GROUP, S5_STATE).transpose(0, 2, 1, 3).reshape(
        S5_NGB, S5_GROUP, S5_Q)
    rows = S5_L * S5_GB * S5_GROUP
    vec = pl.BlockSpec((None, 1, S5_Q), lambda i: (i, 0, 0))
    mat = pl.BlockSpec((None, S5_GROUP, S5_Q), lambda i: (i, 0, 0))
    return pl.pallas_call(
        _s5_param_kernel,
        grid=(S5_NGB,),
        in_specs=[vec, vec, vec, mat, mat, mat, mat],
        out_specs=[pl.BlockSpec((None, rows, 2 * S5_Q), lambda i: (i, 0, 0)),
                   pl.BlockSpec((None, rows, 2 * S5_Q), lambda i: (i, 0, 0)),
                   pl.BlockSpec((None, rows, 2 * S5_GB * S5_GROUP), lambda i: (i, 0, 0)),
                   pl.BlockSpec((None, 1, 2 * S5_Q), lambda i: (i, 0, 0))],
        out_shape=[jax.ShapeDtypeStruct((S5_NGB, rows, 2 * S5_Q), BF16),
                   jax.ShapeDtypeStruct((S5_NGB, rows, 2 * S5_Q), BF16),
                   jax.ShapeDtypeStruct((S5_NGB, rows, 2 * S5_GB * S5_GROUP), BF16),
                   jax.ShapeDtypeStruct((S5_NGB, 1, 2 * S5_Q), F32)],
        compiler_params=_params("parallel"),
        name="s5_operators",
    )(lanes(a_re), lanes(a_im), lanes(ldt), bt(b_re), bt(b_im), ct(c_re), ct(c_im))


def _gelu_tanh(x):
    c = math.sqrt(2.0 / math.pi)
    return 0.5 * x * (1.0 + jnp.tanh(c * (x + 0.044715 * (x * x * x))))


def _s5_kernel(*refs):
    u_refs = refs[:S5_L]
    bs_ref, cs_ref, dr_ref, a_ref, d_ref, o_ref = refs[S5_L:]
    lanes = S5_GB * S5_GROUP
    us = [r[...] for r in u_refs]
    ucat = jnp.concatenate([u.astype(BF16) for u in us], axis=1)
    v = _dot(ucat, bs_ref[...])
    xr, xi = v[:, :S5_Q], v[:, S5_Q:]
    cr, ci = a_ref[:, :S5_Q], a_ref[:, S5_Q:]
    row = lax.broadcasted_iota(jnp.int32, (S5_RB, 1), 0)
    k = 1
    while k < S5_RB:
        sr = jnp.where(row >= k, _shift_rows(xr, k), 0.0)
        si = jnp.where(row >= k, _shift_rows(xi, k), 0.0)
        xr, xi = xr + (cr * sr - ci * si), xi + (cr * si + ci * sr)
        cr, ci = cr * cr - ci * ci, 2.0 * (cr * ci)
        k *= 2
    prev = jnp.concatenate(
        [jnp.where(row >= 1, _shift_rows(xr, 1), 0.0),
         jnp.where(row >= 1, _shift_rows(xi, 1), 0.0)], axis=1).astype(BF16)
    carried = _dot_nt(prev, cs_ref[...])
    d = d_ref[...]
    for m in range(S5_L // 2):
        pair = _dot(ucat[:, :(2 * m + 2) * lanes], dr_ref[(S5_L - 2 - 2 * m) * lanes:, :])
        pair = pair + carried[:, 2 * m * lanes:(2 * m + 2) * lanes]
        for t in (2 * m, 2 * m + 1):
            y = pair[:, (t - 2 * m) * lanes:(t - 2 * m + 1) * lanes] + d * us[t]
            o_ref[t] = _gelu_tanh(y).astype(o_ref.dtype)


def _s5_mixer_core(hn, ops, d):
    bs, cs, dr, a16 = ops
    u2d = hn.reshape(S5_ROWS, S5_L * D_MODEL)
    lanes = S5_GB * S5_GROUP
    rows = S5_L * lanes
    u_specs = [pl.BlockSpec((S5_RB, lanes), lambda g, b, s=s: (b, s * S5_NGB + g))
               for s in range(S5_L)]
    out = pl.pallas_call(
        _s5_kernel,
        grid=(S5_NGB, BATCH),
        in_specs=u_specs + [
            pl.BlockSpec((None, rows, 2 * S5_Q), lambda g, b: (g, 0, 0)),
            pl.BlockSpec((None, rows, 2 * S5_Q), lambda g, b: (g, 0, 0)),
            pl.BlockSpec((None, rows, 2 * lanes), lambda g, b: (g, 0, 0)),
            pl.BlockSpec((None, 1, 2 * S5_Q), lambda g, b: (g, 0, 0)),
            pl.BlockSpec((1, lanes), lambda g, b: (0, g))],
        out_specs=pl.BlockSpec((S5_L, S5_RB, lanes), lambda g, b: (0, b, g)),
        out_shape=jax.ShapeDtypeStruct((S5_L, S5_ROWS, D_MODEL), BF16),
        compiler_params=_params("parallel", "parallel"),
        name="s5_recurrence",
    )(*([u2d] * S5_L), bs, cs, dr, a16, d.reshape(1, D_MODEL))
    return out.transpose(1, 0, 2).reshape(TOKENS, D_MODEL)


def _glu_res_kernel(a_ref, w_ref, r_ref, o_ref):
    y = _dot(a_ref[...], w_ref[...])
    val, gate = y[:, :D_MODEL], y[:, D_MODEL:]
    o_ref[...] = r_ref[...] + val * (1.0 / (1.0 + jnp.exp(-gate)))


def _glu_res(a, w, res):
    return pl.pallas_call(
        _glu_res_kernel,
        grid=(TOKENS // ROW_TILE,),
        in_specs=[pl.BlockSpec((ROW_TILE, D_MODEL), lambda i: (i, 0)),
                  pl.BlockSpec((D_MODEL, 2 * D_MODEL), lambda i: (0, 0)),
                  pl.BlockSpec((ROW_TILE, D_MODEL), lambda i: (i, 0))],
        out_specs=pl.BlockSpec((ROW_TILE, D_MODEL), lambda i: (i, 0)),
        out_shape=jax.ShapeDtypeStruct((TOKENS, D_MODEL), F32),
        compiler_params=_params("parallel"),
        name="s5_glu",
    )(a, w, res)


def _xattn_kernel(h_ref, g_ref, wq_ref, k_ref, v_ref, wo_ref, o_ref):
    x = h_ref[...]
    hn = _rms(x, g_ref[...]).astype(BF16)
    q = (_dot(hn, wq_ref[...]) * (XA_HEAD_DIM ** -0.5)).astype(BF16)
    heads = []
    for hd in range(XA_HEADS):
        sl = slice(hd * XA_HEAD_DIM, (hd + 1) * XA_HEAD_DIM)
        s = _dot_nt(q[:, sl], k_ref[:, sl])
        e = jnp.exp(s - jnp.max(s, axis=-1, keepdims=True))
        denom = jnp.sum(e, axis=-1, keepdims=True)
        heads.append((_dot(e.astype(BF16), v_ref[:, sl]) / denom).astype(BF16))
    o = jnp.concatenate(heads, axis=1)
    o_ref[...] = x + _dot(o, wo_ref[...])


def _xattn_layer(h, g, wq, kv, wo):
    tiles_per_seq = SEQ // ROW_TILE
    return pl.pallas_call(
        _xattn_kernel,
        grid=(TOKENS // ROW_TILE,),
        in_specs=[pl.BlockSpec((ROW_TILE, D_MODEL), lambda i: (i, 0)),
                  pl.BlockSpec((1, D_MODEL), lambda i: (0, 0)),
                  pl.BlockSpec((D_MODEL, D_MODEL), lambda i: (0, 0)),
                  pl.BlockSpec((MEM_LEN, D_MODEL), lambda i: (i // tiles_per_seq, 0)),
                  pl.BlockSpec((MEM_LEN, D_MODEL), lambda i: (i // tiles_per_seq, 1)),
                  pl.BlockSpec((D_MODEL, D_MODEL), lambda i: (0, 0))],
        out_specs=pl.BlockSpec((ROW_TILE, D_MODEL), lambda i: (i, 0)),
        out_shape=jax.ShapeDtypeStruct((TOKENS, D_MODEL), F32),
        compiler_params=_params("parallel"),
        name="mem_xattn",
    )(h, g.reshape(1, D_MODEL), wq, kv, kv, wo)


def _ffn_kernel(h_ref, halo_ref, g_ref, wup_ref, cw_ref, cb_ref, wdn_ref, o_ref, acc_ref):
    i = pl.program_id(0)
    x = h_ref[...]
    g = g_ref[...]
    seq_start = (i % (SEQ // ROW_TILE)) == 0
    halo = jnp.where(seq_start, 0.0, _rms(halo_ref[...], g))
    hn = jnp.concatenate([halo, _rms(x, g)], axis=0).astype(BF16)

    def conv(u, cols):
        cw = cw_ref[:, cols]
        y = cw[2:3] * u + cw[1:2] * _shift_rows(u, 1) + cw[0:1] * _shift_rows(u, 2)
        return y[FFN_HALO:] + cb_ref[:, cols]

    n_chunks = D_FF // FFN_CHUNK
    cols = lambda f, base: slice(base + f * FFN_CHUNK, base + (f + 1) * FFN_CHUNK)
    up = lambda f: (_dot(hn, wup_ref[:, cols(f, 0)]), _dot(hn, wup_ref[:, cols(f, D_FF)]))
    ahead = up(0)
    for f in range(n_chunks):
        vcols, gcols = cols(f, 0), cols(f, D_FF)
        u_val, u_gate = ahead
        if f + 1 < n_chunks:
            ahead = up(f + 1)
        val = conv(u_val, vcols)
        gate = conv(u_gate, gcols)
        act = (gate * (1.0 / (1.0 + jnp.exp(-gate))) * val).astype(BF16)
        part = _dot(act, wdn_ref[vcols, :])
        if f == 0:
            acc_ref[...] = part
        else:
            acc_ref[...] += part
    o_ref[...] = x + acc_ref[...]


def _ffn_layer(h, g, w_up, conv_w, conv_b, w_down):
    halo_blocks = ROW_TILE // FFN_HALO
    resident = dict(pipeline_mode=pl.Buffered(1))
    return pl.pallas_call(
        _ffn_kernel,
        grid=(TOKENS // ROW_TILE,),
        in_specs=[pl.BlockSpec((ROW_TILE, D_MODEL), lambda i: (i, 0)),
                  pl.BlockSpec((FFN_HALO, D_MODEL),
                               lambda i: (jnp.maximum(i * halo_blocks - 1, 0), 0)),
                  pl.BlockSpec((1, D_MODEL), lambda i: (0, 0)),
                  pl.BlockSpec((D_MODEL, 2 * D_FF), lambda i: (0, 0), **resident),
                  pl.BlockSpec((CONV_WIDTH, 2 * D_FF), lambda i: (0, 0)),
                  pl.BlockSpec((1, 2 * D_FF), lambda i: (0, 0)),
                  pl.BlockSpec((D_FF, D_MODEL), lambda i: (0, 0), **resident)],
        out_specs=pl.BlockSpec((ROW_TILE, D_MODEL), lambda i: (i, 0)),
        out_shape=jax.ShapeDtypeStruct((TOKENS, D_MODEL), F32),
        scratch_shapes=[pltpu.VMEM((ROW_TILE, D_MODEL), F32)],
        compiler_params=_params("arbitrary"),
        name="conv_glu_ffn",
    )(h, h, g.reshape(1, D_MODEL), w_up, conv_w, conv_b.reshape(1, 2 * D_FF), w_down)


def kernel(x, mem, mix_norm_g, pool_w, pool_scale, sb_w_qkv, sb_w_o, s5_a_re, s5_a_im, s5_log_dt, s5_b_re, s5_b_im, s5_c_re, s5_c_im, s5_d, s5_w_glu, xa_norm_g, mem_norm_g, xa_wq, xa_wkv, xa_wo, ffn_norm_g, ffn_w_up, ffn_conv_w, ffn_conv_b, ffn_w_down, final_norm_g):
    h = x.reshape(TOKENS, D_MODEL)
    mem2d = mem.reshape(BATCH * MEM_LEN, D_MODEL)
    for i in range(DEPTH):
        kind = i % N_MIXERS
        j = i // N_MIXERS
        if kind == 0:
            h = _pool_layer(h, mix_norm_g[i], pool_w[j], pool_scale[j])
        elif kind == 1:
            qkv = _norm_matmul(h, mix_norm_g[i], sb_w_qkv[j].astype(BF16), "sb_qkv")
            o = _sb_attention(qkv)
            h = _matmul_res(o, sb_w_o[j].astype(BF16), h, "sb_out")
        else:
            ops = _s5_operators(s5_a_re[j], s5_a_im[j], s5_log_dt[j], s5_b_re[j], s5_b_im[j],
                                s5_c_re[j], s5_c_im[j])
            y = _s5_mixer_core(_norm(h, mix_norm_g[i]), ops, s5_d[j])
            h = _glu_res(y, s5_w_glu[j].astype(BF16), h)
        kv = _norm_matmul(mem2d, mem_norm_g[i], xa_wkv[i].astype(BF16), "mem_kv")
        h = _xattn_layer(h, xa_norm_g[i], xa_wq[i].astype(BF16), kv, xa_wo[i].astype(BF16))
        h = _ffn_layer(h, ffn_norm_g[i], ffn_w_up[i].astype(BF16), ffn_conv_w[i], ffn_conv_b[i],
                       ffn_w_down[i].astype(BF16))
    return _norm(h, final_norm_g).reshape(BATCH, SEQ, D_MODEL)
```

```python
import functools
import math

import jax
import jax.numpy as jnp
from jax import lax
from jax.experimental import pallas as pl
from jax.experimental.pallas import tpu as pltpu

F32 = jnp.float32
BF16 = jnp.bfloat16

D_MODEL = 1024
BATCH = 4
SEQ = 4096
TOKENS = BATCH * SEQ
DEPTH = 4
N_MIXERS = 3
EPS = 1e-6

POOL_WINDOWS = (2, 4, 8, 16)
POOL_GROUP = D_MODEL // len(POOL_WINDOWS)
POOL_HALO = 16

SB_HEAD_DIM = 64
SB_TQ = 256
SB_TK = 256
SB_HEADS_PER_STEP = 4
SB_LANES = SB_HEADS_PER_STEP * SB_HEAD_DIM
SB_DEAD = 110.0

S5_GROUP = 16
S5_GROUPS = D_MODEL // S5_GROUP
S5_STATE = 64
S5_L = 16
S5_GB = 8
S5_NGB = S5_GROUPS // S5_GB
S5_Q = S5_GB * S5_STATE
S5_ROWS = TOKENS // S5_L
S5_RB = SEQ // S5_L

MEM_LEN = 256
XA_HEADS = 4
XA_HEAD_DIM = D_MODEL // XA_HEADS

D_FF = 2816
FFN_CHUNK = 256
FFN_HALO = 16
CONV_WIDTH = 3

ROW_TILE = 512
LANES = 128
PANELS = D_MODEL // LANES
FFN_RES = 8
FFN_GROUPS = ROW_TILE // FFN_RES
VMEM_LIMIT = 56 * 1024 * 1024


def _params(*sem):
    return pltpu.CompilerParams(dimension_semantics=sem, vmem_limit_bytes=VMEM_LIMIT)


def _rms(x, g):
    ms = jnp.mean(x * x, axis=-1, keepdims=True)
    return x * lax.rsqrt(ms + EPS) * g


def _dot(a, b):
    return jnp.dot(a, b, preferred_element_type=F32)


def _dot_nt(a, b, precision=None):
    return lax.dot_general(a, b, (((1,), (1,)), ((), ())),
                           preferred_element_type=F32, precision=precision)


def _h_spec(rows, index_map=lambda i: (0, i, 0)):
    return pl.BlockSpec((PANELS, rows, LANES), index_map)


def _load_h(ref):
    return jnp.concatenate([ref[c] for c in range(PANELS)], axis=1)


def _store_h(ref, y):
    for c in range(PANELS):
        ref[c] = y[:, c * LANES:(c + 1) * LANES]


H_SHAPE = jax.ShapeDtypeStruct((PANELS, TOKENS, LANES), F32)


def _shift_rows(x, k):
    return pltpu.roll(x, k, axis=0)


def _norm_kernel(panel_out, h_ref, g_ref, o_ref):
    y = _rms(_load_h(h_ref), g_ref[...])
    if panel_out:
        _store_h(o_ref, y)
    else:
        o_ref[...] = y


def _norm(h, g, panel_out):
    return pl.pallas_call(
        functools.partial(_norm_kernel, panel_out),
        grid=(TOKENS // ROW_TILE,),
        in_specs=[_h_spec(ROW_TILE), pl.BlockSpec((1, D_MODEL), lambda i: (0, 0))],
        out_specs=_h_spec(ROW_TILE) if panel_out else pl.BlockSpec((ROW_TILE, D_MODEL), lambda i: (i, 0)),
        out_shape=H_SHAPE if panel_out else jax.ShapeDtypeStruct((TOKENS, D_MODEL), F32),
        compiler_params=_params("parallel"),
        name="rmsnorm",
    )(h, g.reshape(1, D_MODEL))


def _norm_matmul_kernel(panel_in, x_ref, g_ref, w_ref, o_ref):
    x = _load_h(x_ref) if panel_in else x_ref[...]
    hn = _rms(x, g_ref[...]).astype(BF16)
    o_ref[...] = _dot(hn, w_ref[...]).astype(o_ref.dtype)


def _norm_matmul(x, g, w, name, panel_in):
    rows = x.shape[1] if panel_in else x.shape[0]
    n = w.shape[1]
    return pl.pallas_call(
        functools.partial(_norm_matmul_kernel, panel_in),
        grid=(rows // ROW_TILE,),
        in_specs=[_h_spec(ROW_TILE) if panel_in else pl.BlockSpec((ROW_TILE, D_MODEL), lambda i: (i, 0)),
                  pl.BlockSpec((1, D_MODEL), lambda i: (0, 0)),
                  pl.BlockSpec((D_MODEL, n), lambda i: (0, 0))],
        out_specs=pl.BlockSpec((ROW_TILE, n), lambda i: (i, 0)),
        out_shape=jax.ShapeDtypeStruct((rows, n), BF16),
        compiler_params=_params("parallel"),
        name=name,
    )(x, g.reshape(1, D_MODEL), w)


def _matmul_res_kernel(a_ref, w_ref, r_ref, o_ref):
    _store_h(o_ref, _load_h(r_ref) + _dot(a_ref[...], w_ref[...]))


def _matmul_res(a, w, res, name):
    rows, k = a.shape
    return pl.pallas_call(
        _matmul_res_kernel,
        grid=(rows // ROW_TILE,),
        in_specs=[pl.BlockSpec((ROW_TILE, k), lambda i: (i, 0)),
                  pl.BlockSpec((k, D_MODEL), lambda i: (0, 0)),
                  _h_spec(ROW_TILE)],
        out_specs=_h_spec(ROW_TILE),
        out_shape=H_SHAPE,
        compiler_params=_params("parallel"),
        name=name,
    )(a, w, res)


def _pool_kernel(panel_in, h_ref, halo_ref, g_ref, w_ref, scale_ref, o_ref):
    i = pl.program_id(0)
    tiles_per_seq = SEQ // ROW_TILE
    x = _load_h(h_ref) if panel_in else h_ref[...]
    g = g_ref[...]
    hn = _rms(x, g)
    seq_start = (i % tiles_per_seq) == 0
    halo = _load_h(halo_ref) if panel_in else halo_ref[...]
    halo = jnp.where(seq_start, 0.0, _rms(halo, g))
    ext = jnp.concatenate([halo, hn], axis=0)
    pos = (i % tiles_per_seq) * ROW_TILE + lax.broadcasted_iota(jnp.int32, (ROW_TILE, 1), 0)
    for gi, win in enumerate(POOL_WINDOWS):
        sl = slice(gi * POOL_GROUP, (gi + 1) * POOL_GROUP)
        s = ext[:, sl]
        k = 1
        while k < win:
            s = s + _shift_rows(s, k)
            k *= 2
        cnt = jnp.minimum(pos + 1, win).astype(F32)
        p = s[POOL_HALO:] / cnt - hn[:, sl]
        y = x[:, sl] + _dot(p.astype(BF16), w_ref[gi]) * scale_ref[:, sl]
        for c in range(gi * POOL_GROUP // LANES, (gi + 1) * POOL_GROUP // LANES):
            o_ref[c] = y[:, c * LANES - gi * POOL_GROUP:(c + 1) * LANES - gi * POOL_GROUP]


def _pool_layer(h, g, w, scale, panel_in):
    halo_blocks = ROW_TILE // POOL_HALO
    before = lambda i: jnp.maximum(i * halo_blocks - 1, 0)
    if panel_in:
        h_specs = [_h_spec(ROW_TILE), _h_spec(POOL_HALO, lambda i: (0, before(i), 0))]
    else:
        h_specs = [pl.BlockSpec((ROW_TILE, D_MODEL), lambda i: (i, 0)),
                   pl.BlockSpec((POOL_HALO, D_MODEL), lambda i: (before(i), 0))]
    return pl.pallas_call(
        functools.partial(_pool_kernel, panel_in),
        grid=(TOKENS // ROW_TILE,),
        in_specs=h_specs + [
                  pl.BlockSpec((1, D_MODEL), lambda i: (0, 0)),
                  pl.BlockSpec((len(POOL_WINDOWS), POOL_GROUP, POOL_GROUP), lambda i: (0, 0, 0)),
                  pl.BlockSpec((1, D_MODEL), lambda i: (0, 0))],
        out_specs=_h_spec(ROW_TILE),
        out_shape=H_SHAPE,
        compiler_params=_params("parallel"),
        name="pool_mixer",
    )(h, h, g.reshape(1, D_MODEL), w.astype(BF16), scale.reshape(1, D_MODEL))


def _sb_kernel(q_ref, k_ref, v_ref, o_ref):
    qi = pl.program_id(2)
    q = q_ref[...] * (SB_HEAD_DIM ** -0.5)
    lane = lax.broadcasted_iota(jnp.int32, (1, SB_LANES), 1) // SB_HEAD_DIM
    row = lax.broadcasted_iota(jnp.int32, (SB_TQ, SB_TK), 0)
    col = lax.broadcasted_iota(jnp.int32, (SB_TQ, SB_TK), 1)
    later = (row > col).astype(BF16)
    causal = col < row

    heads = [lane == hh for hh in range(SB_HEADS_PER_STEP)]
    qs = [jnp.where(mine, q, jnp.zeros_like(q)) for mine in heads]

    def rows_of(ref, j):
        return ref[pl.ds(pl.multiple_of(j * SB_TK, SB_TK), SB_TK), :]

    def logits(j):
        kb = rows_of(k_ref, j)
        return [_dot_nt(qh, kb) for qh in qs]

    def values(j):
        vb = rows_of(v_ref, j)
        return jnp.concatenate([jnp.where(mine, vb, jnp.zeros_like(vb)) for mine in heads], axis=0)

    def weights(zs, runs, masked):
        nks = [jnp.maximum(z, 0.0) + jnp.log(1.0 + jnp.exp(-jnp.abs(z))) for z in zs]
        if masked:
            nks = [jnp.where(causal, nk, 0.0) for nk in nks]
        nbs = [_dot(nk.astype(BF16), later) for nk in nks]
        ws = [jnp.exp(z - nk - nb - run) for z, nk, nb, run in zip(zs, nks, nbs, runs)]
        if masked:
            ws = [jnp.where(causal, w, 0.0) for w in ws]
        wcat = jnp.concatenate([w.astype(BF16) for w in ws], axis=1)
        runs = tuple(run + jnp.sum(nk, axis=1, keepdims=True) for run, nk in zip(runs, nks))
        return wcat, runs

    def block(j, carry, masked):
        runs, acc = carry
        wcat, runs = weights(logits(j), runs, masked)
        return runs, acc + _dot(wcat, values(j))

    def step(state):
        n, _, runs, acc = state
        runs, acc = block(qi - 1 - n, (runs, acc), False)
        live = jnp.min(functools.reduce(jnp.minimum, runs)) <= SB_DEAD
        return n + 1, live, runs, acc

    prev = jnp.maximum(qi - 1, 0)
    runs = (jnp.zeros((SB_TQ, 1), F32),) * SB_HEADS_PER_STEP
    z_diag, z_prev = logits(qi), logits(prev)
    w_diag, runs = weights(z_diag, runs, True)
    w_prev, runs = weights(z_prev, runs, False)
    w_prev = jnp.where(qi > 0, w_prev, jnp.zeros_like(w_prev))
    acc = _dot(w_diag, values(qi)) + _dot(w_prev, values(prev))
    live = jnp.min(functools.reduce(jnp.minimum, runs)) <= SB_DEAD
    state = lax.while_loop(lambda s: (s[0] < qi) & s[1], step, (jnp.int32(1), live, runs, acc))
    o_ref[...] = state[3].astype(o_ref.dtype)


def _sb_attention(qkv):
    blocks = D_MODEL // SB_LANES
    qblocks = SEQ // SB_TQ
    return pl.pallas_call(
        _sb_kernel,
        grid=(BATCH, blocks, qblocks),
        in_specs=[pl.BlockSpec((SB_TQ, SB_LANES), lambda b, p, i: (b * qblocks + i, p)),
                  pl.BlockSpec((SEQ, SB_LANES), lambda b, p, i: (b, blocks + p)),
                  pl.BlockSpec((SEQ, SB_LANES), lambda b, p, i: (b, 2 * blocks + p))],
        out_specs=pl.BlockSpec((SB_TQ, SB_LANES), lambda b, p, i: (b * qblocks + i, p)),
        out_shape=jax.ShapeDtypeStruct((TOKENS, D_MODEL), BF16),
        compiler_params=_params("parallel", "parallel", "arbitrary"),
        name="sb_attention",
    )(qkv, qkv, qkv)


def _s5_param_kernel(lr_ref, li_ref, ldt_ref, btr_ref, bti_ref, ctr_ref, cti_ref,
                     bs_ref, cs_ref, dr_ref, a_ref):
    lr = lr_ref[...]
    li = li_ref[...]
    dt = jnp.exp(ldt_ref[...])
    mag = jnp.exp(dt * lr)
    ar = mag * jnp.cos(dt * li)
    ai = mag * jnp.sin(dt * li)
    den = lr * lr + li * li
    cfr = ((ar - 1.0) * lr + ai * li) / den
    cfi = (ai * lr - (ar - 1.0) * li) / den
    btr = btr_ref[...]
    bti = bti_ref[...]
    bbr = cfr * btr - cfi * bti
    bbi = cfr * bti + cfi * btr
    rows = S5_GB * S5_GROUP
    own = (lax.broadcasted_iota(jnp.int32, (rows, S5_Q), 0) // S5_GROUP
           == lax.broadcasted_iota(jnp.int32, (rows, S5_Q), 1) // S5_STATE)
    tile = lambda m: jnp.where(own, jnp.concatenate([m] * S5_GB, axis=0), 0.0)
    bbr, bbi = tile(bbr), tile(bbi)
    ccr, cci = tile(ctr_ref[...]), tile(cti_ref[...])
    pr = [jnp.ones_like(ar)]
    pi = [jnp.zeros_like(ai)]
    for _ in range(S5_L):
        pr.append(pr[-1] * ar - pi[-1] * ai)
        pi.append(pr[-2] * ai + pi[-1] * ar)
    b0 = jnp.concatenate([bbr, bbi], axis=1)
    for s in range(S5_L):
        k = S5_L - 1 - s
        blk = slice(s * rows, (s + 1) * rows)
        bs_ref[blk, :S5_Q] = (pr[k] * bbr - pi[k] * bbi).astype(BF16)
        bs_ref[blk, S5_Q:] = (pr[k] * bbi + pi[k] * bbr).astype(BF16)
    for k in range(S5_L + 1):
        zr = pr[k] * ccr - pi[k] * cci
        zi = pr[k] * cci + pi[k] * ccr
        z = jnp.concatenate([zr, -zi], axis=1)
        if k >= 1:
            cs_ref[(k - 1) * rows:k * rows, :] = z.astype(BF16)
        if k < S5_L:
            dk = _dot_nt(b0, z, precision=lax.Precision.HIGHEST).astype(BF16)
            dr_ref[(S5_L - 1 - k) * rows:(S5_L - k) * rows, rows:] = dk
            if k < S5_L - 1:
                dr_ref[(S5_L - 2 - k) * rows:(S5_L - 1 - k) * rows, :rows] = dk
    dr_ref[(S5_L - 1) * rows:, :rows] = jnp.zeros((rows, rows), BF16)
    a_ref[:, :S5_Q] = pr[S5_L]
    a_ref[:, S5_Q:] = pi[S5_L]


def _s5_operators(a_re, a_im, log_dt, b_re, b_im, c_re, c_im):
    lanes = lambda m: m.reshape(S5_NGB, 1, S5_Q)
    ldt = jnp.broadcast_to(log_dt[:, None], (S5_GROUPS, S5_STATE))
    bt = lambda m: m.reshape(S5_NGB, S5_GB, S5_STATE, S5_GROUP).transpose(0, 3, 1, 2).reshape(
        S5_NGB, S5_GROUP, S5_Q)
    ct = lambda m: m.reshape(S5_NGB, S5_GB, S5_GROUP, S5_STATE).transpose(0, 2, 1, 3).reshape(
        S5_NGB, S5_GROUP, S5_Q)
    rows = S5_L * S5_GB * S5_GROUP
    vec = pl.BlockSpec((None, 1, S5_Q), lambda i: (i, 0, 0))
    mat = pl.BlockSpec((None, S5_GROUP, S5_Q), lambda i: (i, 0, 0))
    return pl.pallas_call(
        _s5_param_kernel,
        grid=(S5_NGB,),
        in_specs=[vec, vec, vec, mat, mat, mat, mat],
        out_specs=[pl.BlockSpec((None, rows, 2 * S5_Q), lambda i: (i, 0, 0)),
                   pl.BlockSpec((None, rows, 2 * S5_Q), lambda i: (i, 0, 0)),
                   pl.BlockSpec((None, rows, 2 * S5_GB * S5_GROUP), lambda i: (i, 0, 0)),
                   pl.BlockSpec((None, 1, 2 * S5_Q), lambda i: (i, 0, 0))],
        out_shape=[jax.ShapeDtypeStruct((S5_NGB, rows, 2 * S5_Q), BF16),
                   jax.ShapeDtypeStruct((S5_NGB, rows, 2 * S5_Q), BF16),
                   jax.ShapeDtypeStruct((S5_NGB, rows, 2 * S5_GB * S5_GROUP), BF16),
                   jax.ShapeDtypeStruct((S5_NGB, 1, 2 * S5_Q), F32)],
        compiler_params=_params("parallel"),
        name="s5_operators",
    )(lanes(a_re), lanes(a_im), lanes(ldt), bt(b_re), bt(b_im), ct(c_re), ct(c_im))


def _gelu_tanh(x):
    c = math.sqrt(2.0 / math.pi)
    return 0.5 * x * (1.0 + jnp.tanh(c * (x + 0.044715 * (x * x * x))))


def _s5_kernel(u_ref, bs_ref, cs_ref, dr_ref, a_ref, d_ref, o_ref):
    lanes = S5_GB * S5_GROUP
    us = [u_ref[pl.ds(s, S5_RB, stride=S5_L), :] for s in range(S5_L)]
    ucat = jnp.concatenate([u.astype(BF16) for u in us], axis=1)
    v = _dot(ucat, bs_ref[...])
    xr, xi = v[:, :S5_Q], v[:, S5_Q:]
    cr, ci = a_ref[:, :S5_Q], a_ref[:, S5_Q:]
    row = lax.broadcasted_iota(jnp.int32, (S5_RB, 1), 0)
    k = 1
    while k < S5_RB:
        sr = jnp.where(row >= k, _shift_rows(xr, k), 0.0)
        si = jnp.where(row >= k, _shift_rows(xi, k), 0.0)
        xr, xi = xr + (cr * sr - ci * si), xi + (cr * si + ci * sr)
        cr, ci = cr * cr - ci * ci, 2.0 * (cr * ci)
        k *= 2
    prev = jnp.concatenate(
        [jnp.where(row >= 1, _shift_rows(xr, 1), 0.0),
         jnp.where(row >= 1, _shift_rows(xi, 1), 0.0)], axis=1).astype(BF16)
    carried = _dot_nt(prev, cs_ref[...])
    d = d_ref[...]
    for m in range(S5_L // 2):
        pair = _dot(ucat[:, :(2 * m + 2) * lanes], dr_ref[(S5_L - 2 - 2 * m) * lanes:, :])
        pair = pair + carried[:, 2 * m * lanes:(2 * m + 2) * lanes]
        for t in (2 * m, 2 * m + 1):
            y = pair[:, (t - 2 * m) * lanes:(t - 2 * m + 1) * lanes] + d * us[t]
            o_ref[t] = _gelu_tanh(y).astype(o_ref.dtype)


def _s5_mixer_core(hn, ops, d):
    bs, cs, dr, a16 = ops
    lanes = S5_GB * S5_GROUP
    rows = S5_L * lanes
    out = pl.pallas_call(
        _s5_kernel,
        grid=(S5_NGB, BATCH),
        in_specs=[
            pl.BlockSpec((None, SEQ, lanes), lambda g, b: (g, b, 0)),
            pl.BlockSpec((None, rows, 2 * S5_Q), lambda g, b: (g, 0, 0)),
            pl.BlockSpec((None, rows, 2 * S5_Q), lambda g, b: (g, 0, 0)),
            pl.BlockSpec((None, rows, 2 * lanes), lambda g, b: (g, 0, 0)),
            pl.BlockSpec((None, 1, 2 * S5_Q), lambda g, b: (g, 0, 0)),
            pl.BlockSpec((1, lanes), lambda g, b: (0, g))],
        out_specs=pl.BlockSpec((S5_L, S5_RB, lanes), lambda g, b: (0, b, g)),
        out_shape=jax.ShapeDtypeStruct((S5_L, S5_ROWS, D_MODEL), BF16),
        compiler_params=_params("parallel", "parallel"),
        name="s5_recurrence",
    )(hn, bs, cs, dr, a16, d.reshape(1, D_MODEL))
    return out.transpose(1, 0, 2).reshape(TOKENS, D_MODEL)


def _glu_res_kernel(a_ref, w_ref, r_ref, o_ref):
    y = _dot(a_ref[...], w_ref[...])
    val, gate = y[:, :D_MODEL], y[:, D_MODEL:]
    _store_h(o_ref, _load_h(r_ref) + val * (1.0 / (1.0 + jnp.exp(-gate))))


def _glu_res(a, w, res):
    return pl.pallas_call(
        _glu_res_kernel,
        grid=(TOKENS // ROW_TILE,),
        in_specs=[pl.BlockSpec((ROW_TILE, D_MODEL), lambda i: (i, 0)),
                  pl.BlockSpec((D_MODEL, 2 * D_MODEL), lambda i: (0, 0)),
                  _h_spec(ROW_TILE)],
        out_specs=_h_spec(ROW_TILE),
        out_shape=H_SHAPE,
        compiler_params=_params("parallel"),
        name="s5_glu",
    )(a, w, res)


def _xattn_kernel(h_ref, g_ref, wq_ref, k_ref, v_ref, wo_ref, o_ref):
    x = _load_h(h_ref)
    hn = _rms(x, g_ref[...]).astype(BF16)
    q = (_dot(hn, wq_ref[...]) * (XA_HEAD_DIM ** -0.5)).astype(BF16)
    heads = []
    for hd in range(XA_HEADS):
        sl = slice(hd * XA_HEAD_DIM, (hd + 1) * XA_HEAD_DIM)
        s = _dot_nt(q[:, sl], k_ref[:, sl])
        e = jnp.exp(s - jnp.max(s, axis=-1, keepdims=True))
        denom = jnp.sum(e, axis=-1, keepdims=True)
        heads.append((_dot(e.astype(BF16), v_ref[:, sl]) / denom).astype(BF16))
    o = jnp.concatenate(heads, axis=1)
    _store_h(o_ref, x + _dot(o, wo_ref[...]))


def _xattn_layer(h, g, wq, kv, wo):
    tiles_per_seq = SEQ // ROW_TILE
    return pl.pallas_call(
        _xattn_kernel,
        grid=(TOKENS // ROW_TILE,),
        in_specs=[_h_spec(ROW_TILE),
                  pl.BlockSpec((1, D_MODEL), lambda i: (0, 0)),
                  pl.BlockSpec((D_MODEL, D_MODEL), lambda i: (0, 0)),
                  pl.BlockSpec((MEM_LEN, D_MODEL), lambda i: (i // tiles_per_seq, 0)),
                  pl.BlockSpec((MEM_LEN, D_MODEL), lambda i: (i // tiles_per_seq, 1)),
                  pl.BlockSpec((D_MODEL, D_MODEL), lambda i: (0, 0))],
        out_specs=_h_spec(ROW_TILE),
        out_shape=H_SHAPE,
        compiler_params=_params("parallel"),
        name="mem_xattn",
    )(h, g.reshape(1, D_MODEL), wq, kv, kv, wo)


def _ffn_kernel(h_ref, halo_ref, g_ref, wup_ref, cw_ref, cb_ref, wdn_ref, o_ref, acc_ref):
    i = pl.program_id(0)
    g = g_ref[...]
    seq_start = (i % (SEQ // ROW_TILE)) == 0
    residue = lambda ref, c, s: ref[c, pl.ds(s, FFN_GROUPS, stride=FFN_RES), :]
    xs = [jnp.concatenate([residue(h_ref, c, s) for c in range(PANELS)], axis=1)
          for s in range(FFN_RES)]
    halo = jnp.where(seq_start, 0.0, _rms(_load_h(halo_ref), g))
    hn = jnp.concatenate([halo] + [_rms(x, g) for x in xs], axis=0).astype(BF16)
    first_group = lax.broadcasted_iota(jnp.int32, (FFN_GROUPS, 1), 0) == 0

    def conv(u, cols):
        cw = cw_ref[:, cols]
        cb = cb_ref[:, cols]
        blk = lambda s: u[FFN_HALO + s * FFN_GROUPS:FFN_HALO + (s + 1) * FFN_GROUPS]
        wrapped = lambda s, halo_row: jnp.where(first_group, u[halo_row:halo_row + 1],
                                                _shift_rows(blk(s), 1))
        b = [wrapped(FFN_RES - 2, FFN_HALO - 2), wrapped(FFN_RES - 1, FFN_HALO - 1)]
        b += [blk(s) for s in range(FFN_RES)]
        return jnp.concatenate(
            [cw[2:3] * b[s + 2] + cw[1:2] * b[s + 1] + cw[0:1] * b[s] + cb for s in range(FFN_RES)],
            axis=0)

    n_chunks = D_FF // FFN_CHUNK
    cols = lambda f, base: slice(base + f * FFN_CHUNK, base + (f + 1) * FFN_CHUNK)
    up = lambda f: (_dot(hn, wup_ref[:, cols(f, 0)]), _dot(hn, wup_ref[:, cols(f, D_FF)]))
    ahead = up(0)
    for f in range(n_chunks):
        vcols, gcols = cols(f, 0), cols(f, D_FF)
        u_val, u_gate = ahead
        if f + 1 < n_chunks:
            ahead = up(f + 1)
        val = conv(u_val, vcols)
        gate = conv(u_gate, gcols)
        act = (gate * (1.0 / (1.0 + jnp.exp(-gate))) * val).astype(BF16)
        part = _dot(act, wdn_ref[vcols, :])
        if f == 0:
            acc_ref[...] = part
        else:
            acc_ref[...] += part
    for s in range(FFN_RES):
        y = xs[s] + acc_ref[s * FFN_GROUPS:(s + 1) * FFN_GROUPS, :]
        for c in range(PANELS):
            o_ref[c, pl.ds(s, FFN_GROUPS, stride=FFN_RES), :] = y[:, c * LANES:(c + 1) * LANES]


def _ffn_layer(h, g, w_up, conv_w, conv_b, w_down):
    halo_blocks = ROW_TILE // FFN_HALO
    resident = dict(pipeline_mode=pl.Buffered(1))
    return pl.pallas_call(
        _ffn_kernel,
        grid=(TOKENS // ROW_TILE,),
        in_specs=[_h_spec(ROW_TILE),
                  _h_spec(FFN_HALO, lambda i: (0, jnp.maximum(i * halo_blocks - 1, 0), 0)),
                  pl.BlockSpec((1, D_MODEL), lambda i: (0, 0)),
                  pl.BlockSpec((D_MODEL, 2 * D_FF), lambda i: (0, 0), **resident),
                  pl.BlockSpec((CONV_WIDTH, 2 * D_FF), lambda i: (0, 0)),
                  pl.BlockSpec((1, 2 * D_FF), lambda i: (0, 0)),
                  pl.BlockSpec((D_FF, D_MODEL), lambda i: (0, 0), **resident)],
        out_specs=_h_spec(ROW_TILE),
        out_shape=H_SHAPE,
        scratch_shapes=[pltpu.VMEM((ROW_TILE, D_MODEL), F32)],
        compiler_params=_params("arbitrary"),
        name="conv_glu_ffn",
    )(h, h, g.reshape(1, D_MODEL), w_up, conv_w, conv_b.reshape(1, 2 * D_FF), w_down)


def kernel(x, mem, mix_norm_g, pool_w, pool_scale, sb_w_qkv, sb_w_o, s5_a_re, s5_a_im, s5_log_dt, s5_b_re, s5_b_im, s5_c_re, s5_c_im, s5_d, s5_w_glu, xa_norm_g, mem_norm_g, xa_wq, xa_wkv, xa_wo, ffn_norm_g, ffn_w_up, ffn_conv_w, ffn_conv_b, ffn_w_down, final_norm_g):
    h = x.reshape(TOKENS, D_MODEL)
    mem2d = mem.reshape(BATCH * MEM_LEN, D_MODEL)
    for i in range(DEPTH):
        kind = i % N_MIXERS
        j = i // N_MIXERS
        if kind == 0:
            h = _pool_layer(h, mix_norm_g[i], pool_w[j], pool_scale[j], panel_in=i > 0)
        elif kind == 1:
            qkv = _norm_matmul(h, mix_norm_g[i], sb_w_qkv[j].astype(BF16), "sb_qkv", panel_in=True)
            o = _sb_attention(qkv)
            h = _matmul_res(o, sb_w_o[j].astype(BF16), h, "sb_out")
        else:
            ops = _s5_operators(s5_a_re[j], s5_a_im[j], s5_log_dt[j], s5_b_re[j], s5_b_im[j],
                                s5_c_re[j], s5_c_im[j])
            y = _s5_mixer_core(_norm(h, mix_norm_g[i], panel_out=True), ops, s5_d[j])
            h = _glu_res(y, s5_w_glu[j].astype(BF16), h)
        kv = _norm_matmul(mem2d, mem_norm_g[i], xa_wkv[i].astype(BF16), "mem_kv", panel_in=False)
        h = _xattn_layer(h, xa_norm_g[i], xa_wq[i].astype(BF16), kv, xa_wo[i].astype(BF16))
        h = _ffn_layer(h, ffn_norm_g[i], ffn_w_up[i].astype(BF16), ffn_conv_w[i], ffn_conv_b[i],
                       ffn_w_down[i].astype(BF16))
    return _norm(h, final_norm_g, panel_out=False).reshape(BATCH, SEQ, D_MODEL)
```

```python
import functools
import math

import jax
import jax.numpy as jnp
from jax import lax
from jax.experimental import pallas as pl
from jax.experimental.pallas import tpu as pltpu

F32 = jnp.float32
BF16 = jnp.bfloat16

D_MODEL = 1024
BATCH = 4
SEQ = 4096
TOKENS = BATCH * SEQ
DEPTH = 4
N_MIXERS = 3
EPS = 1e-6

POOL_WINDOWS = (2, 4, 8, 16)
POOL_GROUP = D_MODEL // len(POOL_WINDOWS)
POOL_HALO = 16

SB_HEAD_DIM = 64
SB_TQ = 256
SB_TK = 256
SB_HEADS_PER_STEP = 4
SB_LANES = SB_HEADS_PER_STEP * SB_HEAD_DIM
SB_DEAD = 110.0
SB_MASKED = -1e30

S5_GROUP = 16
S5_GROUPS = D_MODEL // S5_GROUP
S5_STATE = 64
S5_L = 16
S5_GB = 8
S5_NGB = S5_GROUPS // S5_GB
S5_Q = S5_GB * S5_STATE
S5_ROWS = TOKENS // S5_L
S5_RB = SEQ // S5_L

MEM_LEN = 256
XA_HEADS = 4
XA_HEAD_DIM = D_MODEL // XA_HEADS

D_FF = 2816
FFN_CHUNK = 256
FFN_HALO = 16
CONV_WIDTH = 3

ROW_TILE = 512
LANES = 128
PANELS = D_MODEL // LANES
FFN_RES = 8
FFN_GROUPS = ROW_TILE // FFN_RES
VMEM_LIMIT = 56 * 1024 * 1024


def _params(*sem):
    return pltpu.CompilerParams(dimension_semantics=sem, vmem_limit_bytes=VMEM_LIMIT)


def _rms(x, g):
    ms = jnp.mean(x * x, axis=-1, keepdims=True)
    return x * lax.rsqrt(ms + EPS) * g


def _dot(a, b):
    return jnp.dot(a, b, preferred_element_type=F32)


def _dot_nt(a, b, precision=None):
    return lax.dot_general(a, b, (((1,), (1,)), ((), ())),
                           preferred_element_type=F32, precision=precision)


def _h_spec(rows, index_map=lambda i: (0, i, 0)):
    return pl.BlockSpec((PANELS, rows, LANES), index_map)


def _load_h(ref):
    return jnp.concatenate([ref[c] for c in range(PANELS)], axis=1)


def _store_h(ref, y):
    for c in range(PANELS):
        ref[c] = y[:, c * LANES:(c + 1) * LANES]


H_SHAPE = jax.ShapeDtypeStruct((PANELS, TOKENS, LANES), F32)


def _shift_rows(x, k):
    return pltpu.roll(x, k, axis=0)


def _norm_kernel(panel_out, h_ref, g_ref, o_ref):
    y = _rms(_load_h(h_ref), g_ref[...])
    if panel_out:
        _store_h(o_ref, y)
    else:
        o_ref[...] = y


def _norm(h, g, panel_out):
    return pl.pallas_call(
        functools.partial(_norm_kernel, panel_out),
        grid=(TOKENS // ROW_TILE,),
        in_specs=[_h_spec(ROW_TILE), pl.BlockSpec((1, D_MODEL), lambda i: (0, 0))],
        out_specs=_h_spec(ROW_TILE) if panel_out else pl.BlockSpec((ROW_TILE, D_MODEL), lambda i: (i, 0)),
        out_shape=H_SHAPE if panel_out else jax.ShapeDtypeStruct((TOKENS, D_MODEL), F32),
        compiler_params=_params("parallel"),
        name="rmsnorm",
    )(h, g.reshape(1, D_MODEL))


def _norm_matmul_kernel(panel_in, x_ref, g_ref, w_ref, o_ref):
    x = _load_h(x_ref) if panel_in else x_ref[...]
    hn = _rms(x, g_ref[...]).astype(BF16)
    o_ref[...] = _dot(hn, w_ref[...]).astype(o_ref.dtype)


def _norm_matmul(x, g, w, name, panel_in):
    rows = x.shape[1] if panel_in else x.shape[0]
    n = w.shape[1]
    return pl.pallas_call(
        functools.partial(_norm_matmul_kernel, panel_in),
        grid=(rows // ROW_TILE,),
        in_specs=[_h_spec(ROW_TILE) if panel_in else pl.BlockSpec((ROW_TILE, D_MODEL), lambda i: (i, 0)),
                  pl.BlockSpec((1, D_MODEL), lambda i: (0, 0)),
                  pl.BlockSpec((D_MODEL, n), lambda i: (0, 0))],
        out_specs=pl.BlockSpec((ROW_TILE, n), lambda i: (i, 0)),
        out_shape=jax.ShapeDtypeStruct((rows, n), BF16),
        compiler_params=_params("parallel"),
        name=name,
    )(x, g.reshape(1, D_MODEL), w)


def _pool_rows(i, x, halo, g, w_ref, scale_ref):
    tiles_per_seq = SEQ // ROW_TILE
    hn = _rms(x, g)
    seq_start = (i % tiles_per_seq) == 0
    halo = jnp.where(seq_start, 0.0, _rms(halo, g))
    ext = jnp.concatenate([halo, hn], axis=0)
    pos = (i % tiles_per_seq) * ROW_TILE + lax.broadcasted_iota(jnp.int32, (ROW_TILE, 1), 0)
    out = []
    for gi, win in enumerate(POOL_WINDOWS):
        sl = slice(gi * POOL_GROUP, (gi + 1) * POOL_GROUP)
        s = ext[:, sl]
        k = 1
        while k < win:
            s = s + _shift_rows(s, k)
            k *= 2
        cnt = jnp.minimum(pos + 1, win).astype(F32)
        p = s[POOL_HALO:] / cnt - hn[:, sl]
        out.append(x[:, sl] + _dot(p.astype(BF16), w_ref[gi]) * scale_ref[:, sl])
    return jnp.concatenate(out, axis=1)


def _sb_kernel(q_ref, k_ref, v_ref, o_ref):
    qi = pl.program_id(2)
    q = q_ref[...] * (SB_HEAD_DIM ** -0.5)
    lane = lax.broadcasted_iota(jnp.int32, (1, SB_LANES), 1) // SB_HEAD_DIM
    row = lax.broadcasted_iota(jnp.int32, (SB_TQ, SB_TK), 0)
    col = lax.broadcasted_iota(jnp.int32, (SB_TQ, SB_TK), 1)
    later = (row > col).astype(BF16)

    heads = [lane == hh for hh in range(SB_HEADS_PER_STEP)]
    qs = [jnp.where(mine, q, jnp.zeros_like(q)) for mine in heads]

    def rows_of(ref, j):
        return ref[pl.ds(pl.multiple_of(j * SB_TK, SB_TK), SB_TK), :]

    def per_head(vb):
        return [jnp.where(mine, vb, jnp.zeros_like(vb)) for mine in heads]

    def weights(zs, runs, mask, later):
        if mask is not None:
            zs = [jnp.where(mask, z, SB_MASKED) for z in zs]
        nks = [jnp.maximum(z, 0.0) + jnp.log(1.0 + jnp.exp(-jnp.abs(z))) for z in zs]
        nbs = [_dot(nk.astype(BF16), later) for nk in nks]
        ws = [jnp.exp(z - nk - nb - run) for z, nk, nb, run in zip(zs, nks, nbs, runs)]
        wcat = jnp.concatenate([w.astype(BF16) for w in ws], axis=1)
        runs = tuple(run + jnp.sum(nk, axis=1, keepdims=True) for run, nk in zip(runs, nks))
        return wcat, runs

    def step(state):
        n, _, runs, acc = state
        j = qi - 1 - n
        wcat, runs = weights([_dot_nt(qh, rows_of(k_ref, j)) for qh in qs], runs, None, later)
        acc = acc + _dot(wcat, jnp.concatenate(per_head(rows_of(v_ref, j)), axis=0))
        live = jnp.min(functools.reduce(jnp.minimum, runs)) <= SB_DEAD
        return n + 1, live, runs, acc

    prev = jnp.maximum(qi - 1, 0)
    runs = (jnp.zeros((SB_TQ, 1), F32),) * SB_HEADS_PER_STEP
    w_diag, runs = weights([_dot_nt(qh, rows_of(k_ref, qi)) for qh in qs], runs, col < row, later)
    w_prev, runs = weights([_dot_nt(qh, rows_of(k_ref, prev)) for qh in qs], runs, None, later)
    w_prev = jnp.where(qi > 0, w_prev, jnp.zeros_like(w_prev))
    acc = (_dot(w_diag, jnp.concatenate(per_head(rows_of(v_ref, qi)), axis=0))
           + _dot(w_prev, jnp.concatenate(per_head(rows_of(v_ref, prev)), axis=0)))
    live = jnp.min(functools.reduce(jnp.minimum, runs)) <= SB_DEAD
    state = lax.while_loop(lambda s: (s[0] < qi) & s[1], step, (jnp.int32(1), live, runs, acc))
    o_ref[...] = state[3].astype(o_ref.dtype)


def _sb_attention(qkv):
    blocks = D_MODEL // SB_LANES
    qblocks = SEQ // SB_TQ
    return pl.pallas_call(
        _sb_kernel,
        grid=(BATCH, blocks, qblocks),
        in_specs=[pl.BlockSpec((SB_TQ, SB_LANES), lambda b, p, i: (b * qblocks + i, p)),
                  pl.BlockSpec((SEQ, SB_LANES), lambda b, p, i: (b, blocks + p)),
                  pl.BlockSpec((SEQ, SB_LANES), lambda b, p, i: (b, 2 * blocks + p))],
        out_specs=pl.BlockSpec((SB_TQ, SB_LANES), lambda b, p, i: (b * qblocks + i, p)),
        out_shape=jax.ShapeDtypeStruct((TOKENS, D_MODEL), BF16),
        compiler_params=_params("parallel", "parallel", "arbitrary"),
        name="sb_attention",
    )(qkv, qkv, qkv)


def _s5_param_kernel(lr_ref, li_ref, ldt_ref, btr_ref, bti_ref, ctr_ref, cti_ref,
                     bs_ref, cs_ref, dr_ref, a_ref):
    lr = lr_ref[...]
    li = li_ref[...]
    dt = jnp.exp(ldt_ref[...])
    mag = jnp.exp(dt * lr)
    ar = mag * jnp.cos(dt * li)
    ai = mag * jnp.sin(dt * li)
    den = lr * lr + li * li
    cfr = ((ar - 1.0) * lr + ai * li) / den
    cfi = (ai * lr - (ar - 1.0) * li) / den
    btr = btr_ref[...]
    bti = bti_ref[...]
    bbr = cfr * btr - cfi * bti
    bbi = cfr * bti + cfi * btr
    rows = S5_GB * S5_GROUP
    own = (lax.broadcasted_iota(jnp.int32, (rows, S5_Q), 0) // S5_GROUP
           == lax.broadcasted_iota(jnp.int32, (rows, S5_Q), 1) // S5_STATE)
    tile = lambda m: jnp.where(own, jnp.concatenate([m] * S5_GB, axis=0), 0.0)
    bbr, bbi = tile(bbr), tile(bbi)
    ccr, cci = tile(ctr_ref[...]), tile(cti_ref[...])
    pr = [jnp.ones_like(ar)]
    pi = [jnp.zeros_like(ai)]
    for _ in range(S5_L):
        pr.append(pr[-1] * ar - pi[-1] * ai)
        pi.append(pr[-2] * ai + pi[-1] * ar)
    b0 = jnp.concatenate([bbr, bbi], axis=1)
    for s in range(S5_L):
        k = S5_L - 1 - s
        blk = slice(s * rows, (s + 1) * rows)
        bs_ref[blk, :S5_Q] = (pr[k] * bbr - pi[k] * bbi).astype(BF16)
        bs_ref[blk, S5_Q:] = (pr[k] * bbi + pi[k] * bbr).astype(BF16)
    for k in range(S5_L + 1):
        zr = pr[k] * ccr - pi[k] * cci
        zi = pr[k] * cci + pi[k] * ccr
        z = jnp.concatenate([zr, -zi], axis=1)
        if k >= 1:
            cs_ref[(k - 1) * rows:k * rows, :] = z.astype(BF16)
        if k < S5_L:
            dk = _dot_nt(b0, z, precision=lax.Precision.HIGHEST).astype(BF16)
            dr_ref[(S5_L - 1 - k) * rows:(S5_L - k) * rows, rows:] = dk
            if k < S5_L - 1:
                dr_ref[(S5_L - 2 - k) * rows:(S5_L - 1 - k) * rows, :rows] = dk
    dr_ref[(S5_L - 1) * rows:, :rows] = jnp.zeros((rows, rows), BF16)
    a_ref[:, :S5_Q] = pr[S5_L]
    a_ref[:, S5_Q:] = pi[S5_L]


def _s5_operators(a_re, a_im, log_dt, b_re, b_im, c_re, c_im):
    lanes = lambda m: m.reshape(S5_NGB, 1, S5_Q)
    ldt = jnp.broadcast_to(log_dt[:, None], (S5_GROUPS, S5_STATE))
    bt = lambda m: m.reshape(S5_NGB, S5_GB, S5_STATE, S5_GROUP).transpose(0, 3, 1, 2).reshape(
        S5_NGB, S5_GROUP, S5_Q)
    ct = lambda m: m.reshape(S5_NGB, S5_GB, S5_GROUP, S5_STATE).transpose(0, 2, 1, 3).reshape(
        S5_NGB, S5_GROUP, S5_Q)
    rows = S5_L * S5_GB * S5_GROUP
    vec = pl.BlockSpec((None, 1, S5_Q), lambda i: (i, 0, 0))
    mat = pl.BlockSpec((None, S5_GROUP, S5_Q), lambda i: (i, 0, 0))
    return pl.pallas_call(
        _s5_param_kernel,
        grid=(S5_NGB,),
        in_specs=[vec, vec, vec, mat, mat, mat, mat],
        out_specs=[pl.BlockSpec((None, rows, 2 * S5_Q), lambda i: (i, 0, 0)),
                   pl.BlockSpec((None, rows, 2 * S5_Q), lambda i: (i, 0, 0)),
                   pl.BlockSpec((None, rows, 2 * S5_GB * S5_GROUP), lambda i: (i, 0, 0)),
                   pl.BlockSpec((None, 1, 2 * S5_Q), lambda i: (i, 0, 0))],
        out_shape=[jax.ShapeDtypeStruct((S5_NGB, rows, 2 * S5_Q), BF16),
                   jax.ShapeDtypeStruct((S5_NGB, rows, 2 * S5_Q), BF16),
                   jax.ShapeDtypeStruct((S5_NGB, rows, 2 * S5_GB * S5_GROUP), BF16),
                   jax.ShapeDtypeStruct((S5_NGB, 1, 2 * S5_Q), F32)],
        compiler_params=_params("parallel"),
        name="s5_operators",
    )(lanes(a_re), lanes(a_im), lanes(ldt), bt(b_re), bt(b_im), ct(c_re), ct(c_im))


def _gelu_tanh(x):
    c = math.sqrt(2.0 / math.pi)
    return 0.5 * x * (1.0 + jnp.tanh(c * (x + 0.044715 * (x * x * x))))


def _s5_kernel(u_ref, bs_ref, cs_ref, dr_ref, a_ref, d_ref, o_ref):
    lanes = S5_GB * S5_GROUP
    us = [u_ref[pl.ds(s, S5_RB, stride=S5_L), :] for s in range(S5_L)]
    ucat = jnp.concatenate([u.astype(BF16) for u in us], axis=1)
    v = _dot(ucat, bs_ref[...])
    xr, xi = v[:, :S5_Q], v[:, S5_Q:]
    cr, ci = a_ref[:, :S5_Q], a_ref[:, S5_Q:]
    row = lax.broadcasted_iota(jnp.int32, (S5_RB, 1), 0)
    k = 1
    while k < S5_RB:
        sr = jnp.where(row >= k, _shift_rows(xr, k), 0.0)
        si = jnp.where(row >= k, _shift_rows(xi, k), 0.0)
        xr, xi = xr + (cr * sr - ci * si), xi + (cr * si + ci * sr)
        cr, ci = cr * cr - ci * ci, 2.0 * (cr * ci)
        k *= 2
    prev = jnp.concatenate(
        [jnp.where(row >= 1, _shift_rows(xr, 1), 0.0),
         jnp.where(row >= 1, _shift_rows(xi, 1), 0.0)], axis=1).astype(BF16)
    carried = _dot_nt(prev, cs_ref[...])
    d = d_ref[...]
    for m in range(S5_L // 2):
        pair = _dot(ucat[:, :(2 * m + 2) * lanes], dr_ref[(S5_L - 2 - 2 * m) * lanes:, :])
        pair = pair + carried[:, 2 * m * lanes:(2 * m + 2) * lanes]
        for t in (2 * m, 2 * m + 1):
            y = pair[:, (t - 2 * m) * lanes:(t - 2 * m + 1) * lanes] + d * us[t]
            o_ref[t] = _gelu_tanh(y).astype(o_ref.dtype)


def _s5_mixer_core(hn, ops, d):
    bs, cs, dr, a16 = ops
    lanes = S5_GB * S5_GROUP
    rows = S5_L * lanes
    out = pl.pallas_call(
        _s5_kernel,
        grid=(S5_NGB, BATCH),
        in_specs=[
            pl.BlockSpec((None, SEQ, lanes), lambda g, b: (g, b, 0)),
            pl.BlockSpec((None, rows, 2 * S5_Q), lambda g, b: (g, 0, 0)),
            pl.BlockSpec((None, rows, 2 * S5_Q), lambda g, b: (g, 0, 0)),
            pl.BlockSpec((None, rows, 2 * lanes), lambda g, b: (g, 0, 0)),
            pl.BlockSpec((None, 1, 2 * S5_Q), lambda g, b: (g, 0, 0)),
            pl.BlockSpec((1, lanes), lambda g, b: (0, g))],
        out_specs=pl.BlockSpec((S5_L, S5_RB, lanes), lambda g, b: (0, b, g)),
        out_shape=jax.ShapeDtypeStruct((S5_L, S5_ROWS, D_MODEL), BF16),
        compiler_params=_params("parallel", "parallel"),
        name="s5_recurrence",
    )(hn, bs, cs, dr, a16, d.reshape(1, D_MODEL))
    return out.transpose(1, 0, 2).reshape(TOKENS, D_MODEL)


def _xattn_kernel(mixer, *refs):
    if mixer in ("pool", "pool_first"):
        h_ref, halo_ref, mg_ref, pw_ref, ps_ref = refs[:5]
        refs = refs[5:]
        natural = mixer == "pool_first"
        x = _pool_rows(pl.program_id(0), h_ref[...] if natural else _load_h(h_ref),
                       halo_ref[...] if natural else _load_h(halo_ref), mg_ref[...], pw_ref, ps_ref)
    else:
        a_ref, w_ref, h_ref = refs[:3]
        refs = refs[3:]
        y = _dot(a_ref[...], w_ref[...])
        if mixer == "glu":
            y = y[:, :D_MODEL] * (1.0 / (1.0 + jnp.exp(-y[:, D_MODEL:])))
        x = _load_h(h_ref) + y
    g_ref, wq_ref, k_ref, v_ref, wo_ref, o_ref = refs
    hn = _rms(x, g_ref[...]).astype(BF16)
    q = (_dot(hn, wq_ref[...]) * (XA_HEAD_DIM ** -0.5)).astype(BF16)
    heads = []
    for hd in range(XA_HEADS):
        sl = slice(hd * XA_HEAD_DIM, (hd + 1) * XA_HEAD_DIM)
        s = _dot_nt(q[:, sl], k_ref[:, sl])
        e = jnp.exp(s - jnp.max(s, axis=-1, keepdims=True))
        denom = jnp.sum(e, axis=-1, keepdims=True)
        heads.append((_dot(e.astype(BF16), v_ref[:, sl]) / denom).astype(BF16))
    o = jnp.concatenate(heads, axis=1)
    _store_h(o_ref, x + _dot(o, wo_ref[...]))


def _mixer_xattn_layer(mixer, mixer_args, g, wq, kv, wo):
    tiles_per_seq = SEQ // ROW_TILE
    row = lambda i: (0, 0)
    if mixer in ("pool", "pool_first"):
        h, mg, pw, ps = mixer_args
        before = lambda i: jnp.maximum(i * (ROW_TILE // POOL_HALO) - 1, 0)
        if mixer == "pool":
            h_specs = [_h_spec(ROW_TILE), _h_spec(POOL_HALO, lambda i: (0, before(i), 0))]
        else:
            h_specs = [pl.BlockSpec((ROW_TILE, D_MODEL), lambda i: (i, 0)),
                       pl.BlockSpec((POOL_HALO, D_MODEL), lambda i: (before(i), 0))]
        args = (h, h, mg.reshape(1, D_MODEL), pw.astype(BF16), ps.reshape(1, D_MODEL))
        specs = h_specs + [pl.BlockSpec((1, D_MODEL), row),
                           pl.BlockSpec((len(POOL_WINDOWS), POOL_GROUP, POOL_GROUP), lambda i: (0, 0, 0)),
                           pl.BlockSpec((1, D_MODEL), row)]
    else:
        a, w, h = mixer_args
        args = (a, w, h)
        specs = [pl.BlockSpec((ROW_TILE, a.shape[1]), lambda i: (i, 0)),
                 pl.BlockSpec(w.shape, row), _h_spec(ROW_TILE)]
    return pl.pallas_call(
        functools.partial(_xattn_kernel, mixer),
        grid=(TOKENS // ROW_TILE,),
        in_specs=specs + [
                  pl.BlockSpec((1, D_MODEL), row),
                  pl.BlockSpec((D_MODEL, D_MODEL), row),
                  pl.BlockSpec((MEM_LEN, D_MODEL), lambda i: (i // tiles_per_seq, 0)),
                  pl.BlockSpec((MEM_LEN, D_MODEL), lambda i: (i // tiles_per_seq, 1)),
                  pl.BlockSpec((D_MODEL, D_MODEL), row)],
        out_specs=_h_spec(ROW_TILE),
        out_shape=H_SHAPE,
        compiler_params=_params("parallel"),
        name="mixer_xattn",
    )(*args, g.reshape(1, D_MODEL), wq, kv, kv, wo)


def _ffn_kernel(h_ref, halo_ref, g_ref, wup_ref, cw_ref, cb_ref, wdn_ref, o_ref, acc_ref):
    i = pl.program_id(0)
    g = g_ref[...]
    seq_start = (i % (SEQ // ROW_TILE)) == 0
    residue = lambda ref, c, s: ref[c, pl.ds(s, FFN_GROUPS, stride=FFN_RES), :]
    xs = [jnp.concatenate([residue(h_ref, c, s) for c in range(PANELS)], axis=1)
          for s in range(FFN_RES)]
    halo = jnp.where(seq_start, 0.0, _rms(_load_h(halo_ref), g))
    hn = jnp.concatenate([halo] + [_rms(x, g) for x in xs], axis=0).astype(BF16)
    first_group = lax.broadcasted_iota(jnp.int32, (FFN_GROUPS, 1), 0) == 0

    def conv(u, cols):
        cw = cw_ref[:, cols]
        cb = cb_ref[:, cols]
        blk = lambda s: u[FFN_HALO + s * FFN_GROUPS:FFN_HALO + (s + 1) * FFN_GROUPS]
        wrapped = lambda s, halo_row: jnp.where(first_group, u[halo_row:halo_row + 1],
                                                _shift_rows(blk(s), 1))
        b = [wrapped(FFN_RES - 2, FFN_HALO - 2), wrapped(FFN_RES - 1, FFN_HALO - 1)]
        b += [blk(s) for s in range(FFN_RES)]
        return jnp.concatenate(
            [cw[2:3] * b[s + 2] + cw[1:2] * b[s + 1] + cw[0:1] * b[s] + cb for s in range(FFN_RES)],
            axis=0)

    n_chunks = D_FF // FFN_CHUNK
    cols = lambda f, base: slice(base + f * FFN_CHUNK, base + (f + 1) * FFN_CHUNK)
    up = lambda f: (_dot(hn, wup_ref[:, cols(f, 0)]), _dot(hn, wup_ref[:, cols(f, D_FF)]))
    ahead = up(0)
    for f in range(n_chunks):
        vcols, gcols = cols(f, 0), cols(f, D_FF)
        u_val, u_gate = ahead
        if f + 1 < n_chunks:
            ahead = up(f + 1)
        val = conv(u_val, vcols)
        gate = conv(u_gate, gcols)
        act = (gate * (1.0 / (1.0 + jnp.exp(-gate))) * val).astype(BF16)
        part = _dot(act, wdn_ref[vcols, :])
        if f == 0:
            acc_ref[...] = part
        else:
            acc_ref[...] += part
    for s in range(FFN_RES):
        y = xs[s] + acc_ref[s * FFN_GROUPS:(s + 1) * FFN_GROUPS, :]
        for c in range(PANELS):
            o_ref[c, pl.ds(s, FFN_GROUPS, stride=FFN_RES), :] = y[:, c * LANES:(c + 1) * LANES]


def _ffn_layer(h, g, w_up, conv_w, conv_b, w_down):
    halo_blocks = ROW_TILE // FFN_HALO
    resident = dict(pipeline_mode=pl.Buffered(1))
    return pl.pallas_call(
        _ffn_kernel,
        grid=(TOKENS // ROW_TILE,),
        in_specs=[_h_spec(ROW_TILE),
                  _h_spec(FFN_HALO, lambda i: (0, jnp.maximum(i * halo_blocks - 1, 0), 0)),
                  pl.BlockSpec((1, D_MODEL), lambda i: (0, 0)),
                  pl.BlockSpec((D_MODEL, 2 * D_FF), lambda i: (0, 0), **resident),
                  pl.BlockSpec((CONV_WIDTH, 2 * D_FF), lambda i: (0, 0)),
                  pl.BlockSpec((1, 2 * D_FF), lambda i: (0, 0)),
                  pl.BlockSpec((D_FF, D_MODEL), lambda i: (0, 0), **resident)],
        out_specs=_h_spec(ROW_TILE),
        out_shape=H_SHAPE,
        scratch_shapes=[pltpu.VMEM((ROW_TILE, D_MODEL), F32)],
        compiler_params=_params("arbitrary"),
        name="conv_glu_ffn",
    )(h, h, g.reshape(1, D_MODEL), w_up, conv_w, conv_b.reshape(1, 2 * D_FF), w_down)


def kernel(x, mem, mix_norm_g, pool_w, pool_scale, sb_w_qkv, sb_w_o, s5_a_re, s5_a_im, s5_log_dt, s5_b_re, s5_b_im, s5_c_re, s5_c_im, s5_d, s5_w_glu, xa_norm_g, mem_norm_g, xa_wq, xa_wkv, xa_wo, ffn_norm_g, ffn_w_up, ffn_conv_w, ffn_conv_b, ffn_w_down, final_norm_g):
    h = x.reshape(TOKENS, D_MODEL)
    mem2d = mem.reshape(BATCH * MEM_LEN, D_MODEL)
    for i in range(DEPTH):
        kind = i % N_MIXERS
        j = i // N_MIXERS
        if kind == 0:
            mixer = "pool" if i > 0 else "pool_first"
            mixer_args = (h, mix_norm_g[i], pool_w[j], pool_scale[j])
        elif kind == 1:
            qkv = _norm_matmul(h, mix_norm_g[i], sb_w_qkv[j].astype(BF16), "sb_qkv", panel_in=True)
            mixer, mixer_args = "proj", (_sb_attention(qkv), sb_w_o[j].astype(BF16), h)
        else:
            ops = _s5_operators(s5_a_re[j], s5_a_im[j], s5_log_dt[j], s5_b_re[j], s5_b_im[j],
                                s5_c_re[j], s5_c_im[j])
            y = _s5_mixer_core(_norm(h, mix_norm_g[i], panel_out=True), ops, s5_d[j])
            mixer, mixer_args = "glu", (y, s5_w_glu[j].astype(BF16), h)
        kv = _norm_matmul(mem2d, mem_norm_g[i], xa_wkv[i].astype(BF16), "mem_kv", panel_in=False)
        h = _mixer_xattn_layer(mixer, mixer_args, xa_norm_g[i], xa_wq[i].astype(BF16), kv,
                               xa_wo[i].astype(BF16))
        h = _ffn_layer(h, ffn_norm_g[i], ffn_w_up[i].astype(BF16), ffn_conv_w[i], ffn_conv_b[i],
                       ffn_w_down[i].astype(BF16))
    return _norm(h, final_norm_g, panel_out=False).reshape(BATCH, SEQ, D_MODEL)
```

```python
import functools
import math

import jax
import jax.numpy as jnp
from jax import lax
from jax.experimental import pallas as pl
from jax.experimental.pallas import tpu as pltpu

F32 = jnp.float32
BF16 = jnp.bfloat16

D_MODEL = 1024
BATCH = 4
SEQ = 4096
TOKENS = BATCH * SEQ
DEPTH = 4
N_MIXERS = 3
EPS = 1e-6

POOL_WINDOWS = (2, 4, 8, 16)
POOL_GROUP = D_MODEL // len(POOL_WINDOWS)
POOL_HALO = 16

SB_HEAD_DIM = 64
SB_TQ = 256
SB_TK = 256
SB_HEADS_PER_STEP = 4
SB_LANES = SB_HEADS_PER_STEP * SB_HEAD_DIM
SB_DEAD = 110.0
SB_MASKED = -1e30

S5_GROUP = 16
S5_GROUPS = D_MODEL // S5_GROUP
S5_STATE = 64
S5_L = 16
S5_GB = 8
S5_NGB = S5_GROUPS // S5_GB
S5_Q = S5_GB * S5_STATE
S5_ROWS = TOKENS // S5_L
S5_RB = SEQ // S5_L

MEM_LEN = 256
XA_HEADS = 4
XA_HEAD_DIM = D_MODEL // XA_HEADS

D_FF = 2816
FFN_CHUNK = 256
FFN_HALO = 16
CONV_WIDTH = 3

ROW_TILE = 512
LANES = 128
PANELS = D_MODEL // LANES
FFN_RES = 8
FFN_GROUPS = ROW_TILE // FFN_RES
VMEM_LIMIT = 56 * 1024 * 1024


def _params(*sem):
    return pltpu.CompilerParams(dimension_semantics=sem, vmem_limit_bytes=VMEM_LIMIT)


def _rms(x, g):
    ms = jnp.mean(x * x, axis=-1, keepdims=True)
    return x * lax.rsqrt(ms + EPS) * g


def _dot(a, b):
    return jnp.dot(a, b, preferred_element_type=F32)


def _dot_nt(a, b, precision=None):
    return lax.dot_general(a, b, (((1,), (1,)), ((), ())),
                           preferred_element_type=F32, precision=precision)


def _h_spec(rows, index_map=lambda i: (0, i, 0)):
    return pl.BlockSpec((PANELS, rows, LANES), index_map)


def _load_h(ref):
    return jnp.concatenate([ref[c] for c in range(PANELS)], axis=1)


def _store_h(ref, y):
    for c in range(PANELS):
        ref[c] = y[:, c * LANES:(c + 1) * LANES]


H_SHAPE = jax.ShapeDtypeStruct((PANELS, TOKENS, LANES), F32)


def _shift_rows(x, k):
    return pltpu.roll(x, k, axis=0)


def _norm_kernel(panel_out, h_ref, g_ref, o_ref):
    y = _rms(_load_h(h_ref), g_ref[...])
    if panel_out:
        _store_h(o_ref, y)
    else:
        o_ref[...] = y


def _norm(h, g, panel_out):
    return pl.pallas_call(
        functools.partial(_norm_kernel, panel_out),
        grid=(TOKENS // ROW_TILE,),
        in_specs=[_h_spec(ROW_TILE), pl.BlockSpec((1, D_MODEL), lambda i: (0, 0))],
        out_specs=_h_spec(ROW_TILE) if panel_out else pl.BlockSpec((ROW_TILE, D_MODEL), lambda i: (i, 0)),
        out_shape=H_SHAPE if panel_out else jax.ShapeDtypeStruct((TOKENS, D_MODEL), F32),
        compiler_params=_params("parallel"),
        name="rmsnorm",
    )(h, g.reshape(1, D_MODEL))


def _norm_matmul_kernel(panel_in, x_ref, g_ref, w_ref, o_ref):
    x = _load_h(x_ref) if panel_in else x_ref[...]
    hn = _rms(x, g_ref[...]).astype(BF16)
    o_ref[...] = _dot(hn, w_ref[...]).astype(o_ref.dtype)


def _norm_matmul(x, g, w, name, panel_in):
    rows = x.shape[1] if panel_in else x.shape[0]
    n = w.shape[1]
    return pl.pallas_call(
        functools.partial(_norm_matmul_kernel, panel_in),
        grid=(rows // ROW_TILE,),
        in_specs=[_h_spec(ROW_TILE) if panel_in else pl.BlockSpec((ROW_TILE, D_MODEL), lambda i: (i, 0)),
                  pl.BlockSpec((1, D_MODEL), lambda i: (0, 0)),
                  pl.BlockSpec((D_MODEL, n), lambda i: (0, 0))],
        out_specs=pl.BlockSpec((ROW_TILE, n), lambda i: (i, 0)),
        out_shape=jax.ShapeDtypeStruct((rows, n), BF16),
        compiler_params=_params("parallel"),
        name=name,
    )(x, g.reshape(1, D_MODEL), w)


def _pool_rows(i, x, halo, g, w_ref, scale_ref):
    tiles_per_seq = SEQ // ROW_TILE
    hn = _rms(x, g)
    seq_start = (i % tiles_per_seq) == 0
    halo = jnp.where(seq_start, 0.0, _rms(halo, g))
    ext = jnp.concatenate([halo, hn], axis=0)
    pos = (i % tiles_per_seq) * ROW_TILE + lax.broadcasted_iota(jnp.int32, (ROW_TILE, 1), 0)
    out = []
    for gi, win in enumerate(POOL_WINDOWS):
        sl = slice(gi * POOL_GROUP, (gi + 1) * POOL_GROUP)
        s = ext[:, sl]
        k = 1
        while k < win:
            s = s + _shift_rows(s, k)
            k *= 2
        cnt = jnp.minimum(pos + 1, win).astype(F32)
        p = s[POOL_HALO:] / cnt - hn[:, sl]
        out.append(x[:, sl] + _dot(p.astype(BF16), w_ref[gi]) * scale_ref[:, sl])
    return jnp.concatenate(out, axis=1)


def _sb_kernel(q_ref, k_ref, v_ref, o_ref):
    qi = pl.program_id(2)
    q = q_ref[...] * (SB_HEAD_DIM ** -0.5)
    lane = lax.broadcasted_iota(jnp.int32, (1, SB_LANES), 1) // SB_HEAD_DIM
    row = lax.broadcasted_iota(jnp.int32, (SB_TQ, SB_TK), 0)
    col = lax.broadcasted_iota(jnp.int32, (SB_TQ, SB_TK), 1)
    later = (row > col).astype(BF16)

    heads = [lane == hh for hh in range(SB_HEADS_PER_STEP)]
    qs = [jnp.where(mine, q, jnp.zeros_like(q)) for mine in heads]

    def rows_of(ref, j):
        return ref[pl.ds(pl.multiple_of(j * SB_TK, SB_TK), SB_TK), :]

    def per_head(vb):
        return [jnp.where(mine, vb, jnp.zeros_like(vb)) for mine in heads]

    def weights(zs, runs, mask, later):
        if mask is not None:
            zs = [jnp.where(mask, z, SB_MASKED) for z in zs]
        nks = [jnp.maximum(z, 0.0) + jnp.log(1.0 + jnp.exp(-jnp.abs(z))) for z in zs]
        nbs = [_dot(nk.astype(BF16), later) for nk in nks]
        ws = [jnp.exp(z - nk - nb - run) for z, nk, nb, run in zip(zs, nks, nbs, runs)]
        wcat = jnp.concatenate([w.astype(BF16) for w in ws], axis=1)
        runs = tuple(run + jnp.sum(nk, axis=1, keepdims=True) for run, nk in zip(runs, nks))
        return wcat, runs

    def step(state):
        n, _, runs, acc = state
        j = qi - 1 - n
        wcat, runs = weights([_dot_nt(qh, rows_of(k_ref, j)) for qh in qs], runs, None, later)
        acc = acc + _dot(wcat, jnp.concatenate(per_head(rows_of(v_ref, j)), axis=0))
        live = jnp.min(functools.reduce(jnp.minimum, runs)) <= SB_DEAD
        return n + 1, live, runs, acc

    prev = jnp.maximum(qi - 1, 0)
    runs = (jnp.zeros((SB_TQ, 1), F32),) * SB_HEADS_PER_STEP
    w_diag, runs = weights([_dot_nt(qh, rows_of(k_ref, qi)) for qh in qs], runs, col < row, later)
    w_prev, runs = weights([_dot_nt(qh, rows_of(k_ref, prev)) for qh in qs], runs, None, later)
    w_prev = jnp.where(qi > 0, w_prev, jnp.zeros_like(w_prev))
    acc = (_dot(w_diag, jnp.concatenate(per_head(rows_of(v_ref, qi)), axis=0))
           + _dot(w_prev, jnp.concatenate(per_head(rows_of(v_ref, prev)), axis=0)))
    live = jnp.min(functools.reduce(jnp.minimum, runs)) <= SB_DEAD
    state = lax.while_loop(lambda s: (s[0] < qi) & s[1], step, (jnp.int32(1), live, runs, acc))
    o_ref[...] = state[3].astype(o_ref.dtype)


def _sb_attention(qkv):
    blocks = D_MODEL // SB_LANES
    qblocks = SEQ // SB_TQ
    return pl.pallas_call(
        _sb_kernel,
        grid=(BATCH, blocks, qblocks),
        in_specs=[pl.BlockSpec((SB_TQ, SB_LANES), lambda b, p, i: (b * qblocks + i, p)),
                  pl.BlockSpec((SEQ, SB_LANES), lambda b, p, i: (b, blocks + p)),
                  pl.BlockSpec((SEQ, SB_LANES), lambda b, p, i: (b, 2 * blocks + p))],
        out_specs=pl.BlockSpec((SB_TQ, SB_LANES), lambda b, p, i: (b * qblocks + i, p)),
        out_shape=jax.ShapeDtypeStruct((TOKENS, D_MODEL), BF16),
        compiler_params=_params("parallel", "parallel", "arbitrary"),
        name="sb_attention",
    )(qkv, qkv, qkv)


def _s5_param_kernel(lr_ref, li_ref, ldt_ref, btr_ref, bti_ref, ctr_ref, cti_ref,
                     bs_ref, cs_ref, dr_ref, a_ref):
    lr = lr_ref[...]
    li = li_ref[...]
    dt = jnp.exp(ldt_ref[...])
    mag = jnp.exp(dt * lr)
    ar = mag * jnp.cos(dt * li)
    ai = mag * jnp.sin(dt * li)
    den = lr * lr + li * li
    cfr = ((ar - 1.0) * lr + ai * li) / den
    cfi = (ai * lr - (ar - 1.0) * li) / den
    btr = btr_ref[...]
    bti = bti_ref[...]
    bbr = cfr * btr - cfi * bti
    bbi = cfr * bti + cfi * btr
    rows = S5_GB * S5_GROUP
    own = (lax.broadcasted_iota(jnp.int32, (rows, S5_Q), 0) // S5_GROUP
           == lax.broadcasted_iota(jnp.int32, (rows, S5_Q), 1) // S5_STATE)
    tile = lambda m: jnp.where(own, jnp.concatenate([m] * S5_GB, axis=0), 0.0)
    bbr, bbi = tile(bbr), tile(bbi)
    ccr, cci = tile(ctr_ref[...]), tile(cti_ref[...])
    pr = [jnp.ones_like(ar)]
    pi = [jnp.zeros_like(ai)]
    for _ in range(S5_L):
        pr.append(pr[-1] * ar - pi[-1] * ai)
        pi.append(pr[-2] * ai + pi[-1] * ar)
    b0 = jnp.concatenate([bbr, bbi], axis=1)
    for s in range(S5_L):
        k = S5_L - 1 - s
        blk = slice(s * rows, (s + 1) * rows)
        bs_ref[blk, :S5_Q] = (pr[k] * bbr - pi[k] * bbi).astype(BF16)
        bs_ref[blk, S5_Q:] = (pr[k] * bbi + pi[k] * bbr).astype(BF16)
    for k in range(S5_L + 1):
        zr = pr[k] * ccr - pi[k] * cci
        zi = pr[k] * cci + pi[k] * ccr
        z = jnp.concatenate([zr, -zi], axis=1)
        if k >= 1:
            cs_ref[(k - 1) * rows:k * rows, :] = z.astype(BF16)
        if k < S5_L:
            dk = _dot_nt(b0, z, precision=lax.Precision.HIGHEST).astype(BF16)
            dr_ref[(S5_L - 1 - k) * rows:(S5_L - k) * rows, rows:] = dk
            if k < S5_L - 1:
                dr_ref[(S5_L - 2 - k) * rows:(S5_L - 1 - k) * rows, :rows] = dk
    dr_ref[(S5_L - 1) * rows:, :rows] = jnp.zeros((rows, rows), BF16)
    a_ref[:, :S5_Q] = pr[S5_L]
    a_ref[:, S5_Q:] = pi[S5_L]


def _s5_operators(a_re, a_im, log_dt, b_re, b_im, c_re, c_im):
    lanes = lambda m: m.reshape(S5_NGB, 1, S5_Q)
    ldt = jnp.broadcast_to(log_dt[:, None], (S5_GROUPS, S5_STATE))
    bt = lambda m: m.reshape(S5_NGB, S5_GB, S5_STATE, S5_GROUP).transpose(0, 3, 1, 2).reshape(
        S5_NGB, S5_GROUP, S5_Q)
    ct = lambda m: m.reshape(S5_NGB, S5_GB, S5_GROUP, S5_STATE).transpose(0, 2, 1, 3).reshape(
        S5_NGB, S5_GROUP, S5_Q)
    rows = S5_L * S5_GB * S5_GROUP
    vec = pl.BlockSpec((None, 1, S5_Q), lambda i: (i, 0, 0))
    mat = pl.BlockSpec((None, S5_GROUP, S5_Q), lambda i: (i, 0, 0))
    return pl.pallas_call(
        _s5_param_kernel,
        grid=(S5_NGB,),
        in_specs=[vec, vec, vec, mat, mat, mat, mat],
        out_specs=[pl.BlockSpec((None, rows, 2 * S5_Q), lambda i: (i, 0, 0)),
                   pl.BlockSpec((None, rows, 2 * S5_Q), lambda i: (i, 0, 0)),
                   pl.BlockSpec((None, rows, 2 * S5_GB * S5_GROUP), lambda i: (i, 0, 0)),
                   pl.BlockSpec((None, 1, 2 * S5_Q), lambda i: (i, 0, 0))],
        out_shape=[jax.ShapeDtypeStruct((S5_NGB, rows, 2 * S5_Q), BF16),
                   jax.ShapeDtypeStruct((S5_NGB, rows, 2 * S5_Q), BF16),
                   jax.ShapeDtypeStruct((S5_NGB, rows, 2 * S5_GB * S5_GROUP), BF16),
                   jax.ShapeDtypeStruct((S5_NGB, 1, 2 * S5_Q), F32)],
        compiler_params=_params("parallel"),
        name="s5_operators",
    )(lanes(a_re), lanes(a_im), lanes(ldt), bt(b_re), bt(b_im), ct(c_re), ct(c_im))


def _gelu_tanh(x):
    c = math.sqrt(2.0 / math.pi)
    return 0.5 * x * (1.0 + jnp.tanh(c * (x + 0.044715 * (x * x * x))))


def _s5_kernel(u_ref, bs_ref, cs_ref, dr_ref, a_ref, d_ref, o_ref):
    lanes = S5_GB * S5_GROUP
    us = [u_ref[pl.ds(s, S5_RB, stride=S5_L), :] for s in range(S5_L)]
    ucat = jnp.concatenate([u.astype(BF16) for u in us], axis=1)
    v = _dot(ucat, bs_ref[...])
    xr, xi = v[:, :S5_Q], v[:, S5_Q:]
    cr, ci = a_ref[:, :S5_Q], a_ref[:, S5_Q:]
    row = lax.broadcasted_iota(jnp.int32, (S5_RB, 1), 0)
    k = 1
    while k < S5_RB:
        sr = jnp.where(row >= k, _shift_rows(xr, k), 0.0)
        si = jnp.where(row >= k, _shift_rows(xi, k), 0.0)
        xr, xi = xr + (cr * sr - ci * si), xi + (cr * si + ci * sr)
        cr, ci = cr * cr - ci * ci, 2.0 * (cr * ci)
        k *= 2
    prev = jnp.concatenate(
        [jnp.where(row >= 1, _shift_rows(xr, 1), 0.0),
         jnp.where(row >= 1, _shift_rows(xi, 1), 0.0)], axis=1).astype(BF16)
    carried = _dot_nt(prev, cs_ref[...])
    d = d_ref[...]
    for m in range(S5_L // 2):
        pair = _dot(ucat[:, :(2 * m + 2) * lanes], dr_ref[(S5_L - 2 - 2 * m) * lanes:, :])
        pair = pair + carried[:, 2 * m * lanes:(2 * m + 2) * lanes]
        for t in (2 * m, 2 * m + 1):
            y = pair[:, (t - 2 * m) * lanes:(t - 2 * m + 1) * lanes] + d * us[t]
            o_ref[t] = _gelu_tanh(y).astype(o_ref.dtype)


def _s5_mixer_core(hn, ops, d):
    bs, cs, dr, a16 = ops
    lanes = S5_GB * S5_GROUP
    rows = S5_L * lanes
    out = pl.pallas_call(
        _s5_kernel,
        grid=(S5_NGB, BATCH),
        in_specs=[
            pl.BlockSpec((None, SEQ, lanes), lambda g, b: (g, b, 0)),
            pl.BlockSpec((None, rows, 2 * S5_Q), lambda g, b: (g, 0, 0)),
            pl.BlockSpec((None, rows, 2 * S5_Q), lambda g, b: (g, 0, 0)),
            pl.BlockSpec((None, rows, 2 * lanes), lambda g, b: (g, 0, 0)),
            pl.BlockSpec((None, 1, 2 * S5_Q), lambda g, b: (g, 0, 0)),
            pl.BlockSpec((1, lanes), lambda g, b: (0, g))],
        out_specs=pl.BlockSpec((S5_L, S5_RB, lanes), lambda g, b: (0, b, g)),
        out_shape=jax.ShapeDtypeStruct((S5_L, S5_ROWS, D_MODEL), BF16),
        compiler_params=_params("parallel", "parallel"),
        name="s5_recurrence",
    )(hn, bs, cs, dr, a16, d.reshape(1, D_MODEL))
    return out.transpose(1, 0, 2).reshape(TOKENS, D_MODEL)


def _xattn_kernel(mixer, *refs):
    if mixer in ("pool", "pool_first"):
        h_ref, halo_ref, mg_ref, pw_ref, ps_ref = refs[:5]
        refs = refs[5:]
        natural = mixer == "pool_first"
        x = _pool_rows(pl.program_id(0), h_ref[...] if natural else _load_h(h_ref),
                       halo_ref[...] if natural else _load_h(halo_ref), mg_ref[...], pw_ref, ps_ref)
    else:
        a_ref, w_ref, h_ref = refs[:3]
        refs = refs[3:]
        y = _dot(a_ref[...], w_ref[...])
        if mixer == "glu":
            y = y[:, :D_MODEL] * (1.0 / (1.0 + jnp.exp(-y[:, D_MODEL:])))
        x = _load_h(h_ref) + y
    g_ref, wq_ref, k_ref, v_ref, wo_ref, o_ref = refs
    hn = _rms(x, g_ref[...]).astype(BF16)
    q = (_dot(hn, wq_ref[...]) * (XA_HEAD_DIM ** -0.5)).astype(BF16)
    heads = []
    for hd in range(XA_HEADS):
        sl = slice(hd * XA_HEAD_DIM, (hd + 1) * XA_HEAD_DIM)
        s = _dot_nt(q[:, sl], k_ref[:, sl])
        e = jnp.exp(s - jnp.max(s, axis=-1, keepdims=True))
        denom = jnp.sum(e, axis=-1, keepdims=True)
        heads.append((_dot(e.astype(BF16), v_ref[:, sl]) / denom).astype(BF16))
    o = jnp.concatenate(heads, axis=1)
    _store_h(o_ref, x + _dot(o, wo_ref[...]))


def _mixer_xattn_layer(mixer, mixer_args, g, wq, kv, wo):
    tiles_per_seq = SEQ // ROW_TILE
    row = lambda i: (0, 0)
    if mixer in ("pool", "pool_first"):
        h, mg, pw, ps = mixer_args
        before = lambda i: jnp.maximum(i * (ROW_TILE // POOL_HALO) - 1, 0)
        if mixer == "pool":
            h_specs = [_h_spec(ROW_TILE), _h_spec(POOL_HALO, lambda i: (0, before(i), 0))]
        else:
            h_specs = [pl.BlockSpec((ROW_TILE, D_MODEL), lambda i: (i, 0)),
                       pl.BlockSpec((POOL_HALO, D_MODEL), lambda i: (before(i), 0))]
        args = (h, h, mg.reshape(1, D_MODEL), pw.astype(BF16), ps.reshape(1, D_MODEL))
        specs = h_specs + [pl.BlockSpec((1, D_MODEL), row),
                           pl.BlockSpec((len(POOL_WINDOWS), POOL_GROUP, POOL_GROUP), lambda i: (0, 0, 0)),
                           pl.BlockSpec((1, D_MODEL), row)]
    else:
        a, w, h = mixer_args
        args = (a, w, h)
        specs = [pl.BlockSpec((ROW_TILE, a.shape[1]), lambda i: (i, 0)),
                 pl.BlockSpec(w.shape, row), _h_spec(ROW_TILE)]
    return pl.pallas_call(
        functools.partial(_xattn_kernel, mixer),
        grid=(TOKENS // ROW_TILE,),
        in_specs=specs + [
                  pl.BlockSpec((1, D_MODEL), row),
                  pl.BlockSpec((D_MODEL, D_MODEL), row),
                  pl.BlockSpec((MEM_LEN, D_MODEL), lambda i: (i // tiles_per_seq, 0)),
                  pl.BlockSpec((MEM_LEN, D_MODEL), lambda i: (i // tiles_per_seq, 1)),
                  pl.BlockSpec((D_MODEL, D_MODEL), row)],
        out_specs=_h_spec(ROW_TILE),
        out_shape=H_SHAPE,
        compiler_params=_params("parallel"),
        name="mixer_xattn",
    )(*args, g.reshape(1, D_MODEL), wq, kv, kv, wo)


def _ffn_kernel(with_next_norm, h_ref, halo_ref, g_ref, wup_ref, cw_ref, cb_ref, wdn_ref, *rest):
    if with_next_norm:
        next_g_ref, o_ref, next_ref, act_ref = rest
    else:
        o_ref, act_ref = rest
    i = pl.program_id(0)
    g = g_ref[...]
    seq_start = (i % (SEQ // ROW_TILE)) == 0
    residue = lambda ref, c, s: ref[c, pl.ds(s, FFN_GROUPS, stride=FFN_RES), :]
    xs = [jnp.concatenate([residue(h_ref, c, s) for c in range(PANELS)], axis=1)
          for s in range(FFN_RES)]
    halo = jnp.where(seq_start, 0.0, _rms(_load_h(halo_ref), g))
    hn = jnp.concatenate([halo] + [_rms(x, g) for x in xs], axis=0).astype(BF16)
    first_group = lax.broadcasted_iota(jnp.int32, (FFN_GROUPS, 1), 0) == 0

    def conv(u, cols):
        cw = cw_ref[:, cols]
        cb = cb_ref[:, cols]
        blk = lambda s: u[FFN_HALO + s * FFN_GROUPS:FFN_HALO + (s + 1) * FFN_GROUPS]
        wrapped = lambda s, halo_row: jnp.where(first_group, u[halo_row:halo_row + 1],
                                                _shift_rows(blk(s), 1))
        b = [wrapped(FFN_RES - 2, FFN_HALO - 2), wrapped(FFN_RES - 1, FFN_HALO - 1)]
        b += [blk(s) for s in range(FFN_RES)]
        return jnp.concatenate(
            [cw[2:3] * b[s + 2] + cw[1:2] * b[s + 1] + cw[0:1] * b[s] + cb for s in range(FFN_RES)],
            axis=0)

    n_chunks = D_FF // FFN_CHUNK
    cols = lambda f, base: slice(base + f * FFN_CHUNK, base + (f + 1) * FFN_CHUNK)
    up = lambda f: (_dot(hn, wup_ref[:, cols(f, 0)]), _dot(hn, wup_ref[:, cols(f, D_FF)]))
    ahead = up(0)
    for f in range(n_chunks):
        vcols, gcols = cols(f, 0), cols(f, D_FF)
        u_val, u_gate = ahead
        if f + 1 < n_chunks:
            ahead = up(f + 1)
        val = conv(u_val, vcols)
        gate = conv(u_gate, gcols)
        act_ref[:, vcols] = (gate * (1.0 / (1.0 + jnp.exp(-gate))) * val).astype(BF16)
    down = _dot(act_ref[...], wdn_ref[...])
    for s in range(FFN_RES):
        y = xs[s] + down[s * FFN_GROUPS:(s + 1) * FFN_GROUPS, :]
        outs = [(o_ref, y)] + ([(next_ref, _rms(y, next_g_ref[...]))] if with_next_norm else [])
        for ref, val in outs:
            for c in range(PANELS):
                ref[c, pl.ds(s, FFN_GROUPS, stride=FFN_RES), :] = val[:, c * LANES:(c + 1) * LANES]


def _ffn_layer(h, g, w_up, conv_w, conv_b, w_down, next_g=None):
    halo_blocks = ROW_TILE // FFN_HALO
    resident = dict(pipeline_mode=pl.Buffered(1))
    with_next = next_g is not None
    gain = pl.BlockSpec((1, D_MODEL), lambda i: (0, 0))
    return pl.pallas_call(
        functools.partial(_ffn_kernel, with_next),
        grid=(TOKENS // ROW_TILE,),
        in_specs=[_h_spec(ROW_TILE),
                  _h_spec(FFN_HALO, lambda i: (0, jnp.maximum(i * halo_blocks - 1, 0), 0)),
                  pl.BlockSpec((1, D_MODEL), lambda i: (0, 0)),
                  pl.BlockSpec((D_MODEL, 2 * D_FF), lambda i: (0, 0), **resident),
                  pl.BlockSpec((CONV_WIDTH, 2 * D_FF), lambda i: (0, 0)),
                  pl.BlockSpec((1, 2 * D_FF), lambda i: (0, 0)),
                  pl.BlockSpec((D_FF, D_MODEL), lambda i: (0, 0), **resident)] + [gain] * with_next,
        out_specs=[_h_spec(ROW_TILE)] * 2 if with_next else _h_spec(ROW_TILE),
        out_shape=[H_SHAPE] * 2 if with_next else H_SHAPE,
        scratch_shapes=[pltpu.VMEM((ROW_TILE, D_FF), BF16)],
        compiler_params=_params("arbitrary"),
        name="conv_glu_ffn",
    )(h, h, g.reshape(1, D_MODEL), w_up, conv_w, conv_b.reshape(1, 2 * D_FF), w_down,
      *([next_g.reshape(1, D_MODEL)] if with_next else []))


def kernel(x, mem, mix_norm_g, pool_w, pool_scale, sb_w_qkv, sb_w_o, s5_a_re, s5_a_im, s5_log_dt, s5_b_re, s5_b_im, s5_c_re, s5_c_im, s5_d, s5_w_glu, xa_norm_g, mem_norm_g, xa_wq, xa_wkv, xa_wo, ffn_norm_g, ffn_w_up, ffn_conv_w, ffn_conv_b, ffn_w_down, final_norm_g):
    h = x.reshape(TOKENS, D_MODEL)
    mem2d = mem.reshape(BATCH * MEM_LEN, D_MODEL)
    hn = None
    for i in range(DEPTH):
        kind = i % N_MIXERS
        j = i // N_MIXERS
        if kind == 0:
            mixer = "pool" if i > 0 else "pool_first"
            mixer_args = (h, mix_norm_g[i], pool_w[j], pool_scale[j])
        elif kind == 1:
            qkv = _norm_matmul(h, mix_norm_g[i], sb_w_qkv[j].astype(BF16), "sb_qkv", panel_in=True)
            mixer, mixer_args = "proj", (_sb_attention(qkv), sb_w_o[j].astype(BF16), h)
        else:
            ops = _s5_operators(s5_a_re[j], s5_a_im[j], s5_log_dt[j], s5_b_re[j], s5_b_im[j],
                                s5_c_re[j], s5_c_im[j])
            y = _s5_mixer_core(hn, ops, s5_d[j])
            mixer, mixer_args = "glu", (y, s5_w_glu[j].astype(BF16), h)
        kv = _norm_matmul(mem2d, mem_norm_g[i], xa_wkv[i].astype(BF16), "mem_kv", panel_in=False)
        h = _mixer_xattn_layer(mixer, mixer_args, xa_norm_g[i], xa_wq[i].astype(BF16), kv,
                               xa_wo[i].astype(BF16))
        next_is_s5 = i + 1 < DEPTH and (i + 1) % N_MIXERS == 2
        out = _ffn_layer(h, ffn_norm_g[i], ffn_w_up[i].astype(BF16), ffn_conv_w[i], ffn_conv_b[i],
                         ffn_w_down[i].astype(BF16), mix_norm_g[i + 1] if next_is_s5 else None)
        h, hn = out if next_is_s5 else (out, None)
    return _norm(h, final_norm_g, panel_out=False).reshape(BATCH, SEQ, D_MODEL)
```

```python
import functools
import math

import jax
import jax.numpy as jnp
from jax import lax
from jax.experimental import pallas as pl
from jax.experimental.pallas import tpu as pltpu

F32 = jnp.float32
BF16 = jnp.bfloat16

D_MODEL = 1024
BATCH = 4
SEQ = 4096
TOKENS = BATCH * SEQ
DEPTH = 4
N_MIXERS = 3
EPS = 1e-6

POOL_WINDOWS = (2, 4, 8, 16)
POOL_GROUP = D_MODEL // len(POOL_WINDOWS)
POOL_HALO = 16

SB_HEAD_DIM = 64
SB_TQ = 256
SB_TK = 256
SB_HEADS_PER_STEP = 4
SB_LANES = SB_HEADS_PER_STEP * SB_HEAD_DIM
SB_DEAD = 110.0
SB_MASKED = -1e30

S5_GROUP = 16
S5_GROUPS = D_MODEL // S5_GROUP
S5_STATE = 64
S5_L = 16
S5_GB = 8
S5_NGB = S5_GROUPS // S5_GB
S5_Q = S5_GB * S5_STATE
S5_ROWS = TOKENS // S5_L
S5_RB = SEQ // S5_L

MEM_LEN = 256
XA_HEADS = 4
XA_HEAD_DIM = D_MODEL // XA_HEADS

D_FF = 2816
FFN_CHUNK = 256
FFN_HALO = 16
CONV_WIDTH = 3

ROW_TILE = 512
LANES = 128
PANELS = D_MODEL // LANES
FFN_RES = 8
FFN_GROUPS = ROW_TILE // FFN_RES
VMEM_LIMIT = 56 * 1024 * 1024


def _params(*sem):
    return pltpu.CompilerParams(dimension_semantics=sem, vmem_limit_bytes=VMEM_LIMIT)


def _rms(x, g):
    ms = jnp.mean(x * x, axis=-1, keepdims=True)
    return x * lax.rsqrt(ms + EPS) * g


def _dot(a, b):
    return jnp.dot(a, b, preferred_element_type=F32)


def _dot_nt(a, b, precision=None):
    return lax.dot_general(a, b, (((1,), (1,)), ((), ())),
                           preferred_element_type=F32, precision=precision)


def _h_spec(rows, index_map=lambda i: (0, i, 0)):
    return pl.BlockSpec((PANELS, rows, LANES), index_map)


def _layer_spec(stack, layer, **kw):
    zeros = (0,) * (stack.ndim - 1)
    return pl.BlockSpec((None,) + stack.shape[1:], lambda *_: (layer,) + zeros, **kw)


def _load_h(ref):
    return jnp.concatenate([ref[c] for c in range(PANELS)], axis=1)


def _store_h(ref, y):
    for c in range(PANELS):
        ref[c] = y[:, c * LANES:(c + 1) * LANES]


H_SHAPE = jax.ShapeDtypeStruct((PANELS, TOKENS, LANES), F32)


def _shift_rows(x, k):
    return pltpu.roll(x, k, axis=0)


def _norm_kernel(panel_out, h_ref, g_ref, o_ref):
    y = _rms(_load_h(h_ref), g_ref[...])
    if panel_out:
        _store_h(o_ref, y)
    else:
        o_ref[...] = y


def _norm(h, g, panel_out):
    return pl.pallas_call(
        functools.partial(_norm_kernel, panel_out),
        grid=(TOKENS // ROW_TILE,),
        in_specs=[_h_spec(ROW_TILE), pl.BlockSpec((1, D_MODEL), lambda i: (0, 0))],
        out_specs=_h_spec(ROW_TILE) if panel_out else pl.BlockSpec((ROW_TILE, D_MODEL), lambda i: (i, 0)),
        out_shape=H_SHAPE if panel_out else jax.ShapeDtypeStruct((TOKENS, D_MODEL), F32),
        compiler_params=_params("parallel"),
        name="rmsnorm",
    )(h, g.reshape(1, D_MODEL))


def _norm_matmul_kernel(panel_in, x_ref, g_ref, w_ref, o_ref):
    x = _load_h(x_ref) if panel_in else x_ref[...]
    hn = _rms(x, g_ref[...]).astype(BF16)
    o_ref[...] = _dot(hn, w_ref[...]).astype(o_ref.dtype)


def _norm_matmul(x, g, w, layer, name, panel_in):
    rows = x.shape[1] if panel_in else x.shape[0]
    n = w.shape[2]
    return pl.pallas_call(
        functools.partial(_norm_matmul_kernel, panel_in),
        grid=(rows // ROW_TILE,),
        in_specs=[_h_spec(ROW_TILE) if panel_in else pl.BlockSpec((ROW_TILE, D_MODEL), lambda i: (i, 0)),
                  pl.BlockSpec((1, D_MODEL), lambda i: (0, 0)),
                  _layer_spec(w, layer)],
        out_specs=pl.BlockSpec((ROW_TILE, n), lambda i: (i, 0)),
        out_shape=jax.ShapeDtypeStruct((rows, n), BF16),
        compiler_params=_params("parallel"),
        name=name,
    )(x, g.reshape(1, D_MODEL), w)


def _pool_rows(i, x, halo, g, w_ref, scale_ref):
    tiles_per_seq = SEQ // ROW_TILE
    hn = _rms(x, g)
    seq_start = (i % tiles_per_seq) == 0
    halo = jnp.where(seq_start, 0.0, _rms(halo, g))
    ext = jnp.concatenate([halo, hn], axis=0)
    pos = (i % tiles_per_seq) * ROW_TILE + lax.broadcasted_iota(jnp.int32, (ROW_TILE, 1), 0)
    out = []
    for gi, win in enumerate(POOL_WINDOWS):
        sl = slice(gi * POOL_GROUP, (gi + 1) * POOL_GROUP)
        s = ext[:, sl]
        k = 1
        while k < win:
            s = s + _shift_rows(s, k)
            k *= 2
        cnt = jnp.minimum(pos + 1, win).astype(F32)
        p = s[POOL_HALO:] / cnt - hn[:, sl]
        out.append(x[:, sl] + _dot(p.astype(BF16), w_ref[gi]) * scale_ref[:, sl])
    return jnp.concatenate(out, axis=1)


def _sb_kernel(q_ref, k_ref, v_ref, o_ref):
    qi = pl.program_id(2)
    q = q_ref[...] * (SB_HEAD_DIM ** -0.5)
    lane = lax.broadcasted_iota(jnp.int32, (1, SB_LANES), 1) // SB_HEAD_DIM
    row = lax.broadcasted_iota(jnp.int32, (SB_TQ, SB_TK), 0)
    col = lax.broadcasted_iota(jnp.int32, (SB_TQ, SB_TK), 1)
    later = (row > col).astype(BF16)

    heads = [lane == hh for hh in range(SB_HEADS_PER_STEP)]
    qs = [jnp.where(mine, q, jnp.zeros_like(q)) for mine in heads]

    def rows_of(ref, j):
        return ref[pl.ds(pl.multiple_of(j * SB_TK, SB_TK), SB_TK), :]

    def per_head(vb):
        return [jnp.where(mine, vb, jnp.zeros_like(vb)) for mine in heads]

    def weights(zs, runs, mask, later):
        if mask is not None:
            zs = [jnp.where(mask, z, SB_MASKED) for z in zs]
        nks = [jnp.maximum(z, 0.0) + jnp.log(1.0 + jnp.exp(-jnp.abs(z))) for z in zs]
        nbs = [_dot(nk.astype(BF16), later) for nk in nks]
        ws = [jnp.exp(z - nk - nb - run) for z, nk, nb, run in zip(zs, nks, nbs, runs)]
        wcat = jnp.concatenate([w.astype(BF16) for w in ws], axis=1)
        runs = tuple(run + jnp.sum(nk, axis=1, keepdims=True) for run, nk in zip(runs, nks))
        return wcat, runs

    def step(state):
        n, _, runs, acc = state
        j = qi - 1 - n
        wcat, runs = weights([_dot_nt(qh, rows_of(k_ref, j)) for qh in qs], runs, None, later)
        acc = acc + _dot(wcat, jnp.concatenate(per_head(rows_of(v_ref, j)), axis=0))
        live = jnp.min(functools.reduce(jnp.minimum, runs)) <= SB_DEAD
        return n + 1, live, runs, acc

    prev = jnp.maximum(qi - 1, 0)
    runs = (jnp.zeros((SB_TQ, 1), F32),) * SB_HEADS_PER_STEP
    w_diag, runs = weights([_dot_nt(qh, rows_of(k_ref, qi)) for qh in qs], runs, col < row, later)
    w_prev, runs = weights([_dot_nt(qh, rows_of(k_ref, prev)) for qh in qs], runs, None, later)
    w_prev = jnp.where(qi > 0, w_prev, jnp.zeros_like(w_prev))
    acc = (_dot(w_diag, jnp.concatenate(per_head(rows_of(v_ref, qi)), axis=0))
           + _dot(w_prev, jnp.concatenate(per_head(rows_of(v_ref, prev)), axis=0)))
    live = jnp.min(functools.reduce(jnp.minimum, runs)) <= SB_DEAD
    state = lax.while_loop(lambda s: (s[0] < qi) & s[1], step, (jnp.int32(1), live, runs, acc))
    o_ref[...] = state[3].astype(o_ref.dtype)


def _sb_attention(qkv):
    blocks = D_MODEL // SB_LANES
    qblocks = SEQ // SB_TQ
    return pl.pallas_call(
        _sb_kernel,
        grid=(BATCH, blocks, qblocks),
        in_specs=[pl.BlockSpec((SB_TQ, SB_LANES), lambda b, p, i: (b * qblocks + i, p)),
                  pl.BlockSpec((SEQ, SB_LANES), lambda b, p, i: (b, blocks + p)),
                  pl.BlockSpec((SEQ, SB_LANES), lambda b, p, i: (b, 2 * blocks + p))],
        out_specs=pl.BlockSpec((SB_TQ, SB_LANES), lambda b, p, i: (b * qblocks + i, p)),
        out_shape=jax.ShapeDtypeStruct((TOKENS, D_MODEL), BF16),
        compiler_params=_params("parallel", "parallel", "arbitrary"),
        name="sb_attention",
    )(qkv, qkv, qkv)


def _s5_param_kernel(lr_ref, li_ref, ldt_ref, btr_ref, bti_ref, ctr_ref, cti_ref,
                     bs_ref, cs_ref, dr_ref, a_ref):
    lr = lr_ref[...]
    li = li_ref[...]
    dt = jnp.exp(ldt_ref[...])
    mag = jnp.exp(dt * lr)
    ar = mag * jnp.cos(dt * li)
    ai = mag * jnp.sin(dt * li)
    den = lr * lr + li * li
    cfr = ((ar - 1.0) * lr + ai * li) / den
    cfi = (ai * lr - (ar - 1.0) * li) / den
    btr = btr_ref[...]
    bti = bti_ref[...]
    bbr = cfr * btr - cfi * bti
    bbi = cfr * bti + cfi * btr
    rows = S5_GB * S5_GROUP
    own = (lax.broadcasted_iota(jnp.int32, (rows, S5_Q), 0) // S5_GROUP
           == lax.broadcasted_iota(jnp.int32, (rows, S5_Q), 1) // S5_STATE)
    tile = lambda m: jnp.where(own, jnp.concatenate([m] * S5_GB, axis=0), 0.0)
    bbr, bbi = tile(bbr), tile(bbi)
    ccr, cci = tile(ctr_ref[...]), tile(cti_ref[...])
    pr = [jnp.ones_like(ar)]
    pi = [jnp.zeros_like(ai)]
    for _ in range(S5_L):
        pr.append(pr[-1] * ar - pi[-1] * ai)
        pi.append(pr[-2] * ai + pi[-1] * ar)
    b0 = jnp.concatenate([bbr, bbi], axis=1)
    for s in range(S5_L):
        k = S5_L - 1 - s
        blk = slice(s * rows, (s + 1) * rows)
        bs_ref[blk, :S5_Q] = (pr[k] * bbr - pi[k] * bbi).astype(BF16)
        bs_ref[blk, S5_Q:] = (pr[k] * bbi + pi[k] * bbr).astype(BF16)
    for k in range(S5_L + 1):
        zr = pr[k] * ccr - pi[k] * cci
        zi = pr[k] * cci + pi[k] * ccr
        z = jnp.concatenate([zr, -zi], axis=1)
        if k >= 1:
            cs_ref[(k - 1) * rows:k * rows, :] = z.astype(BF16)
        if k < S5_L:
            dk = _dot_nt(b0, z, precision=lax.Precision.HIGHEST).astype(BF16)
            dr_ref[(S5_L - 1 - k) * rows:(S5_L - k) * rows, rows:] = dk
            if k < S5_L - 1:
                dr_ref[(S5_L - 2 - k) * rows:(S5_L - 1 - k) * rows, :rows] = dk
    dr_ref[(S5_L - 1) * rows:, :rows] = jnp.zeros((rows, rows), BF16)
    a_ref[:, :S5_Q] = pr[S5_L]
    a_ref[:, S5_Q:] = pi[S5_L]


def _s5_operators(a_re, a_im, log_dt, b_re, b_im, c_re, c_im):
    lanes = lambda m: m.reshape(S5_NGB, 1, S5_Q)
    ldt = jnp.broadcast_to(log_dt[:, None], (S5_GROUPS, S5_STATE))
    bt = lambda m: m.reshape(S5_NGB, S5_GB, S5_STATE, S5_GROUP).transpose(0, 3, 1, 2).reshape(
        S5_NGB, S5_GROUP, S5_Q)
    ct = lambda m: m.reshape(S5_NGB, S5_GB, S5_GROUP, S5_STATE).transpose(0, 2, 1, 3).reshape(
        S5_NGB, S5_GROUP, S5_Q)
    rows = S5_L * S5_GB * S5_GROUP
    vec = pl.BlockSpec((None, 1, S5_Q), lambda i: (i, 0, 0))
    mat = pl.BlockSpec((None, S5_GROUP, S5_Q), lambda i: (i, 0, 0))
    return pl.pallas_call(
        _s5_param_kernel,
        grid=(S5_NGB,),
        in_specs=[vec, vec, vec, mat, mat, mat, mat],
        out_specs=[pl.BlockSpec((None, rows, 2 * S5_Q), lambda i: (i, 0, 0)),
                   pl.BlockSpec((None, rows, 2 * S5_Q), lambda i: (i, 0, 0)),
                   pl.BlockSpec((None, rows, 2 * S5_GB * S5_GROUP), lambda i: (i, 0, 0)),
                   pl.BlockSpec((None, 1, 2 * S5_Q), lambda i: (i, 0, 0))],
        out_shape=[jax.ShapeDtypeStruct((S5_NGB, rows, 2 * S5_Q), BF16),
                   jax.ShapeDtypeStruct((S5_NGB, rows, 2 * S5_Q), BF16),
                   jax.ShapeDtypeStruct((S5_NGB, rows, 2 * S5_GB * S5_GROUP), BF16),
                   jax.ShapeDtypeStruct((S5_NGB, 1, 2 * S5_Q), F32)],
        compiler_params=_params("parallel"),
        name="s5_operators",
    )(lanes(a_re), lanes(a_im), lanes(ldt), bt(b_re), bt(b_im), ct(c_re), ct(c_im))


def _gelu_tanh(x):
    c = math.sqrt(2.0 / math.pi)
    return 0.5 * x * (1.0 + jnp.tanh(c * (x + 0.044715 * (x * x * x))))


def _s5_kernel(u_ref, bs_ref, cs_ref, dr_ref, a_ref, d_ref, o_ref):
    lanes = S5_GB * S5_GROUP
    us = [u_ref[pl.ds(s, S5_RB, stride=S5_L), :] for s in range(S5_L)]
    ucat = jnp.concatenate([u.astype(BF16) for u in us], axis=1)
    v = _dot(ucat, bs_ref[...])
    xr, xi = v[:, :S5_Q], v[:, S5_Q:]
    cr, ci = a_ref[:, :S5_Q], a_ref[:, S5_Q:]
    row = lax.broadcasted_iota(jnp.int32, (S5_RB, 1), 0)
    k = 1
    while k < S5_RB:
        sr = jnp.where(row >= k, _shift_rows(xr, k), 0.0)
        si = jnp.where(row >= k, _shift_rows(xi, k), 0.0)
        xr, xi = xr + (cr * sr - ci * si), xi + (cr * si + ci * sr)
        cr, ci = cr * cr - ci * ci, 2.0 * (cr * ci)
        k *= 2
    prev = jnp.concatenate(
        [jnp.where(row >= 1, _shift_rows(xr, 1), 0.0),
         jnp.where(row >= 1, _shift_rows(xi, 1), 0.0)], axis=1).astype(BF16)
    carried = _dot_nt(prev, cs_ref[...])
    d = d_ref[...]
    for m in range(S5_L // 2):
        pair = _dot(ucat[:, :(2 * m + 2) * lanes], dr_ref[(S5_L - 2 - 2 * m) * lanes:, :])
        pair = pair + carried[:, 2 * m * lanes:(2 * m + 2) * lanes]
        for t in (2 * m, 2 * m + 1):
            y = pair[:, (t - 2 * m) * lanes:(t - 2 * m + 1) * lanes] + d * us[t]
            o_ref[t] = _gelu_tanh(y).astype(o_ref.dtype)


def _s5_mixer_core(hn, ops, d):
    bs, cs, dr, a16 = ops
    lanes = S5_GB * S5_GROUP
    rows = S5_L * lanes
    out = pl.pallas_call(
        _s5_kernel,
        grid=(S5_NGB, BATCH),
        in_specs=[
            pl.BlockSpec((None, SEQ, lanes), lambda g, b: (g, b, 0)),
            pl.BlockSpec((None, rows, 2 * S5_Q), lambda g, b: (g, 0, 0)),
            pl.BlockSpec((None, rows, 2 * S5_Q), lambda g, b: (g, 0, 0)),
            pl.BlockSpec((None, rows, 2 * lanes), lambda g, b: (g, 0, 0)),
            pl.BlockSpec((None, 1, 2 * S5_Q), lambda g, b: (g, 0, 0)),
            pl.BlockSpec((1, lanes), lambda g, b: (0, g))],
        out_specs=pl.BlockSpec((S5_L, S5_RB, lanes), lambda g, b: (0, b, g)),
        out_shape=jax.ShapeDtypeStruct((S5_L, S5_ROWS, D_MODEL), BF16),
        compiler_params=_params("parallel", "parallel"),
        name="s5_recurrence",
    )(hn, bs, cs, dr, a16, d.reshape(1, D_MODEL))
    return out.transpose(1, 0, 2).reshape(TOKENS, D_MODEL)


def _xattn_kernel(mixer, *refs):
    if mixer in ("pool", "pool_first"):
        h_ref, halo_ref, mg_ref, pw_ref, ps_ref = refs[:5]
        refs = refs[5:]
        natural = mixer == "pool_first"
        x = _pool_rows(pl.program_id(0), h_ref[...] if natural else _load_h(h_ref),
                       halo_ref[...] if natural else _load_h(halo_ref), mg_ref[...], pw_ref, ps_ref)
    else:
        a_ref, w_ref, h_ref = refs[:3]
        refs = refs[3:]
        y = _dot(a_ref[...], w_ref[...])
        if mixer == "glu":
            y = y[:, :D_MODEL] * (1.0 / (1.0 + jnp.exp(-y[:, D_MODEL:])))
        x = _load_h(h_ref) + y
    g_ref, wq_ref, k_ref, v_ref, wo_ref, o_ref = refs
    hn = _rms(x, g_ref[...]).astype(BF16)
    q = (_dot(hn, wq_ref[...]) * (XA_HEAD_DIM ** -0.5)).astype(BF16)
    heads = []
    for hd in range(XA_HEADS):
        sl = slice(hd * XA_HEAD_DIM, (hd + 1) * XA_HEAD_DIM)
        s = _dot_nt(q[:, sl], k_ref[:, sl])
        e = jnp.exp(s - jnp.max(s, axis=-1, keepdims=True))
        denom = jnp.sum(e, axis=-1, keepdims=True)
        heads.append((_dot(e.astype(BF16), v_ref[:, sl]) / denom).astype(BF16))
    o = jnp.concatenate(heads, axis=1)
    _store_h(o_ref, x + _dot(o, wo_ref[...]))


def _mixer_xattn_layer(mixer, mixer_args, g, wq, kv, wo, layer):
    tiles_per_seq = SEQ // ROW_TILE
    row = lambda i: (0, 0)
    if mixer in ("pool", "pool_first"):
        h, mg, (pw, pj), ps = mixer_args
        before = lambda i: jnp.maximum(i * (ROW_TILE // POOL_HALO) - 1, 0)
        if mixer == "pool":
            h_specs = [_h_spec(ROW_TILE), _h_spec(POOL_HALO, lambda i: (0, before(i), 0))]
        else:
            h_specs = [pl.BlockSpec((ROW_TILE, D_MODEL), lambda i: (i, 0)),
                       pl.BlockSpec((POOL_HALO, D_MODEL), lambda i: (before(i), 0))]
        args = (h, h, mg.reshape(1, D_MODEL), pw, ps.reshape(1, D_MODEL))
        specs = h_specs + [pl.BlockSpec((1, D_MODEL), row), _layer_spec(pw, pj),
                           pl.BlockSpec((1, D_MODEL), row)]
    else:
        a, (w, wj), h = mixer_args
        args = (a, w, h)
        specs = [pl.BlockSpec((ROW_TILE, a.shape[1]), lambda i: (i, 0)),
                 _layer_spec(w, wj), _h_spec(ROW_TILE)]
    return pl.pallas_call(
        functools.partial(_xattn_kernel, mixer),
        grid=(TOKENS // ROW_TILE,),
        in_specs=specs + [
                  pl.BlockSpec((1, D_MODEL), row),
                  _layer_spec(wq, layer),
                  pl.BlockSpec((MEM_LEN, D_MODEL), lambda i: (i // tiles_per_seq, 0)),
                  pl.BlockSpec((MEM_LEN, D_MODEL), lambda i: (i // tiles_per_seq, 1)),
                  _layer_spec(wo, layer)],
        out_specs=_h_spec(ROW_TILE),
        out_shape=H_SHAPE,
        compiler_params=_params("parallel"),
        name="mixer_xattn",
    )(*args, g.reshape(1, D_MODEL), wq, kv, kv, wo)


def _ffn_kernel(with_next_norm, h_ref, halo_ref, g_ref, wup_ref, cw_ref, cb_ref, wdn_ref, *rest):
    if with_next_norm:
        next_g_ref, o_ref, next_ref, act_ref = rest
    else:
        o_ref, act_ref = rest
    i = pl.program_id(0)
    g = g_ref[...]
    seq_start = (i % (SEQ // ROW_TILE)) == 0
    residue = lambda ref, c, s: ref[c, pl.ds(s, FFN_GROUPS, stride=FFN_RES), :]
    xs = [jnp.concatenate([residue(h_ref, c, s) for c in range(PANELS)], axis=1)
          for s in range(FFN_RES)]
    halo = jnp.where(seq_start, 0.0, _rms(_load_h(halo_ref), g))
    hn = jnp.concatenate([halo] + [_rms(x, g) for x in xs], axis=0).astype(BF16)
    first_group = lax.broadcasted_iota(jnp.int32, (FFN_GROUPS, 1), 0) == 0

    def conv(u, cols):
        cw = cw_ref[:, cols]
        cb = cb_ref[:, cols]
        blk = lambda s: u[FFN_HALO + s * FFN_GROUPS:FFN_HALO + (s + 1) * FFN_GROUPS]
        wrapped = lambda s, halo_row: jnp.where(first_group, u[halo_row:halo_row + 1],
                                                _shift_rows(blk(s), 1))
        b = [wrapped(FFN_RES - 2, FFN_HALO - 2), wrapped(FFN_RES - 1, FFN_HALO - 1)]
        b += [blk(s) for s in range(FFN_RES)]
        return jnp.concatenate(
            [cw[2:3] * b[s + 2] + cw[1:2] * b[s + 1] + cw[0:1] * b[s] + cb for s in range(FFN_RES)],
            axis=0)

    n_chunks = D_FF // FFN_CHUNK
    cols = lambda f, base: slice(base + f * FFN_CHUNK, base + (f + 1) * FFN_CHUNK)
    up = lambda f: (_dot(hn, wup_ref[:, cols(f, 0)]), _dot(hn, wup_ref[:, cols(f, D_FF)]))
    ahead = up(0)
    for f in range(n_chunks):
        vcols, gcols = cols(f, 0), cols(f, D_FF)
        u_val, u_gate = ahead
        if f + 1 < n_chunks:
            ahead = up(f + 1)
        val = conv(u_val, vcols)
        gate = conv(u_gate, gcols)
        act_ref[:, vcols] = (gate * (1.0 / (1.0 + jnp.exp(-gate))) * val).astype(BF16)
    down = _dot(act_ref[...], wdn_ref[...])
    for s in range(FFN_RES):
        y = xs[s] + down[s * FFN_GROUPS:(s + 1) * FFN_GROUPS, :]
        outs = [(o_ref, y)] + ([(next_ref, _rms(y, next_g_ref[...]))] if with_next_norm else [])
        for ref, val in outs:
            for c in range(PANELS):
                ref[c, pl.ds(s, FFN_GROUPS, stride=FFN_RES), :] = val[:, c * LANES:(c + 1) * LANES]


def _ffn_layer(h, g, w_up, conv_w, conv_b, w_down, layer, next_g=None):
    halo_blocks = ROW_TILE // FFN_HALO
    resident = dict(pipeline_mode=pl.Buffered(1))
    with_next = next_g is not None
    gain = pl.BlockSpec((1, D_MODEL), lambda i: (0, 0))
    return pl.pallas_call(
        functools.partial(_ffn_kernel, with_next),
        grid=(TOKENS // ROW_TILE,),
        in_specs=[_h_spec(ROW_TILE),
                  _h_spec(FFN_HALO, lambda i: (0, jnp.maximum(i * halo_blocks - 1, 0), 0)),
                  pl.BlockSpec((1, D_MODEL), lambda i: (0, 0)),
                  _layer_spec(w_up, layer, **resident),
                  pl.BlockSpec((CONV_WIDTH, 2 * D_FF), lambda i: (0, 0)),
                  pl.BlockSpec((1, 2 * D_FF), lambda i: (0, 0)),
                  _layer_spec(w_down, layer, **resident)] + [gain] * with_next,
        out_specs=[_h_spec(ROW_TILE)] * 2 if with_next else _h_spec(ROW_TILE),
        out_shape=[H_SHAPE] * 2 if with_next else H_SHAPE,
        scratch_shapes=[pltpu.VMEM((ROW_TILE, D_FF), BF16)],
        compiler_params=_params("arbitrary"),
        name="conv_glu_ffn",
    )(h, h, g.reshape(1, D_MODEL), w_up, conv_w, conv_b.reshape(1, 2 * D_FF), w_down,
      *([next_g.reshape(1, D_MODEL)] if with_next else []))


def kernel(x, mem, mix_norm_g, pool_w, pool_scale, sb_w_qkv, sb_w_o, s5_a_re, s5_a_im, s5_log_dt, s5_b_re, s5_b_im, s5_c_re, s5_c_im, s5_d, s5_w_glu, xa_norm_g, mem_norm_g, xa_wq, xa_wkv, xa_wo, ffn_norm_g, ffn_w_up, ffn_conv_w, ffn_conv_b, ffn_w_down, final_norm_g):
    h = x.reshape(TOKENS, D_MODEL)
    mem2d = mem.reshape(BATCH * MEM_LEN, D_MODEL)
    pool_w, sb_w_qkv, sb_w_o, s5_w_glu, xa_wq, xa_wkv, xa_wo, ffn_w_up, ffn_w_down = (
        w.astype(BF16) for w in (pool_w, sb_w_qkv, sb_w_o, s5_w_glu, xa_wq, xa_wkv, xa_wo,
                                 ffn_w_up, ffn_w_down))
    hn = None
    for i in range(DEPTH):
        kind = i % N_MIXERS
        j = i // N_MIXERS
        if kind == 0:
            mixer = "pool" if i > 0 else "pool_first"
            mixer_args = (h, mix_norm_g[i], (pool_w, j), pool_scale[j])
        elif kind == 1:
            qkv = _norm_matmul(h, mix_norm_g[i], sb_w_qkv, j, "sb_qkv", panel_in=True)
            mixer, mixer_args = "proj", (_sb_attention(qkv), (sb_w_o, j), h)
        else:
            ops = _s5_operators(s5_a_re[j], s5_a_im[j], s5_log_dt[j], s5_b_re[j], s5_b_im[j],
                                s5_c_re[j], s5_c_im[j])
            y = _s5_mixer_core(hn, ops, s5_d[j])
            mixer, mixer_args = "glu", (y, (s5_w_glu, j), h)
        kv = _norm_matmul(mem2d, mem_norm_g[i], xa_wkv, i, "mem_kv", panel_in=False)
        h = _mixer_xattn_layer(mixer, mixer_args, xa_norm_g[i], xa_wq, kv, xa_wo, i)
        next_is_s5 = i + 1 < DEPTH and (i + 1) % N_MIXERS == 2
        out = _ffn_layer(h, ffn_norm_g[i], ffn_w_up, ffn_conv_w[i], ffn_conv_b[i], ffn_w_down, i,
                         mix_norm_g[i + 1] if next_is_s5 else None)
        h, hn = out if next_is_s5 else (out, None)
    return _norm(h, final_norm_g, panel_out=False).reshape(BATCH, SEQ, D_MODEL)
```

```python
import functools
import math

import jax
import jax.numpy as jnp
from jax import lax
from jax.experimental import pallas as pl
from jax.experimental.pallas import tpu as pltpu

F32 = jnp.float32
BF16 = jnp.bfloat16

D_MODEL = 1024
BATCH = 4
SEQ = 4096
TOKENS = BATCH * SEQ
DEPTH = 4
N_MIXERS = 3
EPS = 1e-6

POOL_WINDOWS = (2, 4, 8, 16)
POOL_GROUP = D_MODEL // len(POOL_WINDOWS)
POOL_HALO = 16

SB_HEAD_DIM = 64
SB_TQ = 256
SB_TK = 256
SB_HEADS_PER_STEP = 4
SB_LANES = SB_HEADS_PER_STEP * SB_HEAD_DIM
SB_DEAD = 110.0
SB_MASKED = -1e30

S5_GROUP = 16
S5_GROUPS = D_MODEL // S5_GROUP
S5_STATE = 64
S5_L = 16
S5_GB = 8
S5_NGB = S5_GROUPS // S5_GB
S5_Q = S5_GB * S5_STATE
S5_ROWS = TOKENS // S5_L
S5_RB = SEQ // S5_L

MEM_LEN = 256
XA_HEADS = 4
XA_HEAD_DIM = D_MODEL // XA_HEADS

D_FF = 2816
FFN_CHUNK = 256
FFN_HALO = 16
CONV_WIDTH = 3

ROW_TILE = 512
LANES = 128
PANELS = D_MODEL // LANES
FFN_RES = 8
FFN_GROUPS = ROW_TILE // FFN_RES
VMEM_LIMIT = 56 * 1024 * 1024


def _params(*sem):
    return pltpu.CompilerParams(dimension_semantics=sem, vmem_limit_bytes=VMEM_LIMIT)


def _rms(x, g):
    ms = jnp.mean(x * x, axis=-1, keepdims=True)
    return x * lax.rsqrt(ms + EPS) * g


def _dot(a, b):
    return jnp.dot(a, b, preferred_element_type=F32)


def _dot_nt(a, b, precision=None):
    return lax.dot_general(a, b, (((1,), (1,)), ((), ())),
                           preferred_element_type=F32, precision=precision)


def _h_spec(rows, index_map=lambda i: (0, i, 0)):
    return pl.BlockSpec((PANELS, rows, LANES), index_map)


def _layer_spec(stack, layer, **kw):
    zeros = (0,) * (stack.ndim - 1)
    return pl.BlockSpec((None,) + stack.shape[1:], lambda *_: (layer,) + zeros, **kw)


def _load_h(ref):
    return jnp.concatenate([ref[c] for c in range(PANELS)], axis=1)


def _store_h(ref, y):
    for c in range(PANELS):
        ref[c] = y[:, c * LANES:(c + 1) * LANES]


H_SHAPE = jax.ShapeDtypeStruct((PANELS, TOKENS, LANES), F32)


def _shift_rows(x, k):
    return pltpu.roll(x, k, axis=0)


def _norm_kernel(panel_out, h_ref, g_ref, o_ref):
    y = _rms(_load_h(h_ref), g_ref[...])
    if panel_out:
        _store_h(o_ref, y)
    else:
        o_ref[...] = y


def _norm(h, g, panel_out):
    return pl.pallas_call(
        functools.partial(_norm_kernel, panel_out),
        grid=(TOKENS // ROW_TILE,),
        in_specs=[_h_spec(ROW_TILE), pl.BlockSpec((1, D_MODEL), lambda i: (0, 0))],
        out_specs=_h_spec(ROW_TILE) if panel_out else pl.BlockSpec((ROW_TILE, D_MODEL), lambda i: (i, 0)),
        out_shape=H_SHAPE if panel_out else jax.ShapeDtypeStruct((TOKENS, D_MODEL), F32),
        compiler_params=_params("parallel"),
        name="rmsnorm",
    )(h, g.reshape(1, D_MODEL))


def _norm_matmul_kernel(panel_in, x_ref, g_ref, w_ref, o_ref):
    x = _load_h(x_ref) if panel_in else x_ref[...]
    hn = _rms(x, g_ref[...]).astype(BF16)
    o_ref[...] = _dot(hn, w_ref[...]).astype(o_ref.dtype)


def _norm_matmul(x, g, w, layer, name, panel_in):
    rows = x.shape[1] if panel_in else x.shape[0]
    n = w.shape[2]
    return pl.pallas_call(
        functools.partial(_norm_matmul_kernel, panel_in),
        grid=(rows // ROW_TILE,),
        in_specs=[_h_spec(ROW_TILE) if panel_in else pl.BlockSpec((ROW_TILE, D_MODEL), lambda i: (i, 0)),
                  pl.BlockSpec((1, D_MODEL), lambda i: (0, 0)),
                  _layer_spec(w, layer)],
        out_specs=pl.BlockSpec((ROW_TILE, n), lambda i: (i, 0)),
        out_shape=jax.ShapeDtypeStruct((rows, n), BF16),
        compiler_params=_params("parallel"),
        name=name,
    )(x, g.reshape(1, D_MODEL), w)


def _pool_rows(i, x, halo, g, w_ref, scale_ref):
    tiles_per_seq = SEQ // ROW_TILE
    hn = _rms(x, g)
    seq_start = (i % tiles_per_seq) == 0
    halo = jnp.where(seq_start, 0.0, _rms(halo, g))
    ext = jnp.concatenate([halo, hn], axis=0)
    pos = (i % tiles_per_seq) * ROW_TILE + lax.broadcasted_iota(jnp.int32, (ROW_TILE, 1), 0)
    out = []
    for gi, win in enumerate(POOL_WINDOWS):
        sl = slice(gi * POOL_GROUP, (gi + 1) * POOL_GROUP)
        s = ext[:, sl]
        k = 1
        while k < win:
            s = s + _shift_rows(s, k)
            k *= 2
        cnt = jnp.minimum(pos + 1, win).astype(F32)
        p = s[POOL_HALO:] / cnt - hn[:, sl]
        out.append(x[:, sl] + _dot(p.astype(BF16), w_ref[gi]) * scale_ref[:, sl])
    return jnp.concatenate(out, axis=1)


def _sb_kernel(q_ref, k_ref, v_ref, o_ref):
    qi = pl.program_id(2)
    q = q_ref[...] * (SB_HEAD_DIM ** -0.5)
    lane = lax.broadcasted_iota(jnp.int32, (1, SB_LANES), 1) // SB_HEAD_DIM
    row = lax.broadcasted_iota(jnp.int32, (SB_TQ, SB_TK), 0)
    col = lax.broadcasted_iota(jnp.int32, (SB_TQ, SB_TK), 1)
    later = (row > col).astype(BF16)

    heads = [lane == hh for hh in range(SB_HEADS_PER_STEP)]
    qs = [jnp.where(mine, q, jnp.zeros_like(q)) for mine in heads]

    def rows_of(ref, j):
        return ref[pl.ds(pl.multiple_of(j * SB_TK, SB_TK), SB_TK), :]

    def per_head(vb):
        return [jnp.where(mine, vb, jnp.zeros_like(vb)) for mine in heads]

    def weights(zs, runs, mask, later):
        if mask is not None:
            zs = [jnp.where(mask, z, SB_MASKED) for z in zs]
        nks = [jnp.maximum(z, 0.0) + jnp.log(1.0 + jnp.exp(-jnp.abs(z))) for z in zs]
        nbs = [_dot(nk.astype(BF16), later) for nk in nks]
        ws = [jnp.exp(z - nk - nb - run) for z, nk, nb, run in zip(zs, nks, nbs, runs)]
        wcat = jnp.concatenate([w.astype(BF16) for w in ws], axis=1)
        runs = tuple(run + jnp.sum(nk, axis=1, keepdims=True) for run, nk in zip(runs, nks))
        return wcat, runs

    def step(state):
        n, _, runs, acc = state
        j = qi - 1 - n
        wcat, runs = weights([_dot_nt(qh, rows_of(k_ref, j)) for qh in qs], runs, None, later)
        acc = acc + _dot(wcat, jnp.concatenate(per_head(rows_of(v_ref, j)), axis=0))
        live = jnp.min(functools.reduce(jnp.minimum, runs)) <= SB_DEAD
        return n + 1, live, runs, acc

    prev = jnp.maximum(qi - 1, 0)
    runs = (jnp.zeros((SB_TQ, 1), F32),) * SB_HEADS_PER_STEP
    w_diag, runs = weights([_dot_nt(qh, rows_of(k_ref, qi)) for qh in qs], runs, col < row, later)
    w_prev, runs = weights([_dot_nt(qh, rows_of(k_ref, prev)) for qh in qs], runs, None, later)
    w_prev = jnp.where(qi > 0, w_prev, jnp.zeros_like(w_prev))
    acc = (_dot(w_diag, jnp.concatenate(per_head(rows_of(v_ref, qi)), axis=0))
           + _dot(w_prev, jnp.concatenate(per_head(rows_of(v_ref, prev)), axis=0)))
    live = jnp.min(functools.reduce(jnp.minimum, runs)) <= SB_DEAD
    state = lax.while_loop(lambda s: (s[0] < qi) & s[1], step, (jnp.int32(1), live, runs, acc))
    o_ref[...] = state[3].astype(o_ref.dtype)


def _sb_attention(qkv):
    blocks = D_MODEL // SB_LANES
    qblocks = SEQ // SB_TQ
    return pl.pallas_call(
        _sb_kernel,
        grid=(BATCH, blocks, qblocks),
        in_specs=[pl.BlockSpec((SB_TQ, SB_LANES), lambda b, p, i: (b * qblocks + i, p)),
                  pl.BlockSpec((SEQ, SB_LANES), lambda b, p, i: (b, blocks + p)),
                  pl.BlockSpec((SEQ, SB_LANES), lambda b, p, i: (b, 2 * blocks + p))],
        out_specs=pl.BlockSpec((SB_TQ, SB_LANES), lambda b, p, i: (b * qblocks + i, p)),
        out_shape=jax.ShapeDtypeStruct((TOKENS, D_MODEL), BF16),
        compiler_params=_params("parallel", "parallel", "arbitrary"),
        name="sb_attention",
    )(qkv, qkv, qkv)


def _s5_param_kernel(lr_ref, li_ref, ldt_ref, btr_ref, bti_ref, ctr_ref, cti_ref,
                     bs_ref, cs_ref, dr_ref, a_ref):
    lr = lr_ref[...]
    li = li_ref[...]
    dt = jnp.exp(ldt_ref[...])
    mag = jnp.exp(dt * lr)
    ar = mag * jnp.cos(dt * li)
    ai = mag * jnp.sin(dt * li)
    den = lr * lr + li * li
    cfr = ((ar - 1.0) * lr + ai * li) / den
    cfi = (ai * lr - (ar - 1.0) * li) / den
    btr = btr_ref[...]
    bti = bti_ref[...]
    bbr = cfr * btr - cfi * bti
    bbi = cfr * bti + cfi * btr
    rows = S5_GB * S5_GROUP
    own = (lax.broadcasted_iota(jnp.int32, (rows, S5_Q), 0) // S5_GROUP
           == lax.broadcasted_iota(jnp.int32, (rows, S5_Q), 1) // S5_STATE)
    tile = lambda m: jnp.where(own, jnp.concatenate([m] * S5_GB, axis=0), 0.0)
    bbr, bbi = tile(bbr), tile(bbi)
    ccr, cci = tile(ctr_ref[...]), tile(cti_ref[...])
    pr = [jnp.ones_like(ar)]
    pi = [jnp.zeros_like(ai)]
    for _ in range(S5_L):
        pr.append(pr[-1] * ar - pi[-1] * ai)
        pi.append(pr[-2] * ai + pi[-1] * ar)
    b0 = jnp.concatenate([bbr, bbi], axis=1)
    for s in range(S5_L):
        k = S5_L - 1 - s
        blk = slice(s * rows, (s + 1) * rows)
        bs_ref[blk, :S5_Q] = (pr[k] * bbr - pi[k] * bbi).astype(BF16)
        bs_ref[blk, S5_Q:] = (pr[k] * bbi + pi[k] * bbr).astype(BF16)
    for k in range(S5_L + 1):
        zr = pr[k] * ccr - pi[k] * cci
        zi = pr[k] * cci + pi[k] * ccr
        z = jnp.concatenate([zr, -zi], axis=1)
        if k >= 1:
            cs_ref[(k - 1) * rows:k * rows, :] = z.astype(BF16)
        if k < S5_L:
            dk = _dot_nt(b0, z, precision=lax.Precision.HIGHEST).astype(BF16)
            dr_ref[(S5_L - 1 - k) * rows:(S5_L - k) * rows, rows:] = dk
            if k < S5_L - 1:
                dr_ref[(S5_L - 2 - k) * rows:(S5_L - 1 - k) * rows, :rows] = dk
    dr_ref[(S5_L - 1) * rows:, :rows] = jnp.zeros((rows, rows), BF16)
    a_ref[:, :S5_Q] = pr[S5_L]
    a_ref[:, S5_Q:] = pi[S5_L]


def _s5_operators(a_re, a_im, log_dt, b_re, b_im, c_re, c_im):
    lanes = lambda m: m.reshape(S5_NGB, 1, S5_Q)
    ldt = jnp.broadcast_to(log_dt[:, None], (S5_GROUPS, S5_STATE))
    bt = lambda m: m.reshape(S5_NGB, S5_GB, S5_STATE, S5_GROUP).transpose(0, 3, 1, 2).reshape(
        S5_NGB, S5_GROUP, S5_Q)
    ct = lambda m: m.reshape(S5_NGB, S5_GB, S5_GROUP, S5_STATE).transpose(0, 2, 1, 3).reshape(
        S5_NGB, S5_GROUP, S5_Q)
    rows = S5_L * S5_GB * S5_GROUP
    vec = pl.BlockSpec((None, 1, S5_Q), lambda i: (i, 0, 0))
    mat = pl.BlockSpec((None, S5_GROUP, S5_Q), lambda i: (i, 0, 0))
    return pl.pallas_call(
        _s5_param_kernel,
        grid=(S5_NGB,),
        in_specs=[vec, vec, vec, mat, mat, mat, mat],
        out_specs=[pl.BlockSpec((None, rows, 2 * S5_Q), lambda i: (i, 0, 0)),
                   pl.BlockSpec((None, rows, 2 * S5_Q), lambda i: (i, 0, 0)),
                   pl.BlockSpec((None, rows, 2 * S5_GB * S5_GROUP), lambda i: (i, 0, 0)),
                   pl.BlockSpec((None, 1, 2 * S5_Q), lambda i: (i, 0, 0))],
        out_shape=[jax.ShapeDtypeStruct((S5_NGB, rows, 2 * S5_Q), BF16),
                   jax.ShapeDtypeStruct((S5_NGB, rows, 2 * S5_Q), BF16),
                   jax.ShapeDtypeStruct((S5_NGB, rows, 2 * S5_GB * S5_GROUP), BF16),
                   jax.ShapeDtypeStruct((S5_NGB, 1, 2 * S5_Q), F32)],
        compiler_params=_params("parallel"),
        name="s5_operators",
    )(lanes(a_re), lanes(a_im), lanes(ldt), bt(b_re), bt(b_im), ct(c_re), ct(c_im))


def _gelu_tanh(x):
    c = math.sqrt(2.0 / math.pi)
    return 0.5 * x * (1.0 + jnp.tanh(c * (x + 0.044715 * (x * x * x))))


def _s5_kernel(u_ref, bs_ref, cs_ref, dr_ref, a_ref, d_ref, o_ref):
    lanes = S5_GB * S5_GROUP
    us = [u_ref[pl.ds(s, S5_RB, stride=S5_L), :] for s in range(S5_L)]
    ucat = jnp.concatenate([u.astype(BF16) for u in us], axis=1)
    v = _dot(ucat, bs_ref[...])
    xr, xi = v[:, :S5_Q], v[:, S5_Q:]
    cr, ci = a_ref[:, :S5_Q], a_ref[:, S5_Q:]
    row = lax.broadcasted_iota(jnp.int32, (S5_RB, 1), 0)
    k = 1
    while k < S5_RB:
        sr = jnp.where(row >= k, _shift_rows(xr, k), 0.0)
        si = jnp.where(row >= k, _shift_rows(xi, k), 0.0)
        xr, xi = xr + (cr * sr - ci * si), xi + (cr * si + ci * sr)
        cr, ci = cr * cr - ci * ci, 2.0 * (cr * ci)
        k *= 2
    prev = jnp.concatenate(
        [jnp.where(row >= 1, _shift_rows(xr, 1), 0.0),
         jnp.where(row >= 1, _shift_rows(xi, 1), 0.0)], axis=1).astype(BF16)
    carried = _dot_nt(prev, cs_ref[...])
    d = d_ref[...]
    for m in range(S5_L // 2):
        pair = _dot(ucat[:, :(2 * m + 2) * lanes], dr_ref[(S5_L - 2 - 2 * m) * lanes:, :])
        pair = pair + carried[:, 2 * m * lanes:(2 * m + 2) * lanes]
        for t in (2 * m, 2 * m + 1):
            y = pair[:, (t - 2 * m) * lanes:(t - 2 * m + 1) * lanes] + d * us[t]
            o_ref[t] = _gelu_tanh(y).astype(o_ref.dtype)


def _s5_mixer_core(hn, ops, d):
    bs, cs, dr, a16 = ops
    lanes = S5_GB * S5_GROUP
    rows = S5_L * lanes
    out = pl.pallas_call(
        _s5_kernel,
        grid=(S5_NGB, BATCH),
        in_specs=[
            pl.BlockSpec((None, SEQ, lanes), lambda g, b: (g, b, 0)),
            pl.BlockSpec((None, rows, 2 * S5_Q), lambda g, b: (g, 0, 0)),
            pl.BlockSpec((None, rows, 2 * S5_Q), lambda g, b: (g, 0, 0)),
            pl.BlockSpec((None, rows, 2 * lanes), lambda g, b: (g, 0, 0)),
            pl.BlockSpec((None, 1, 2 * S5_Q), lambda g, b: (g, 0, 0)),
            pl.BlockSpec((1, lanes), lambda g, b: (0, g))],
        out_specs=pl.BlockSpec((S5_L, S5_RB, lanes), lambda g, b: (0, b, g)),
        out_shape=jax.ShapeDtypeStruct((S5_L, S5_ROWS, D_MODEL), BF16),
        compiler_params=_params("parallel", "parallel"),
        name="s5_recurrence",
    )(hn, bs, cs, dr, a16, d.reshape(1, D_MODEL))
    return out.transpose(1, 0, 2).reshape(TOKENS, D_MODEL)


def _memory_kv_kernel(m_ref, g_ref, w_ref, o_ref):
    o_ref[...] = _dot(_rms(m_ref[...], g_ref[...]).astype(BF16), w_ref[...]).astype(o_ref.dtype)


def _memory_kv(mem2d, gains, wkv):
    rows = mem2d.shape[0]
    return pl.pallas_call(
        _memory_kv_kernel,
        grid=(DEPTH, rows // ROW_TILE),
        in_specs=[pl.BlockSpec((ROW_TILE, D_MODEL), lambda l, i: (i, 0)),
                  pl.BlockSpec((None, 1, D_MODEL), lambda l, i: (l, 0, 0)),
                  pl.BlockSpec((None, D_MODEL, 2 * D_MODEL), lambda l, i: (l, 0, 0))],
        out_specs=pl.BlockSpec((None, ROW_TILE, 2 * D_MODEL), lambda l, i: (l, i, 0)),
        out_shape=jax.ShapeDtypeStruct((DEPTH, rows, 2 * D_MODEL), BF16),
        compiler_params=_params("parallel", "parallel"),
        name="mem_kv",
    )(mem2d, gains.reshape(DEPTH, 1, D_MODEL), wkv)


def _xattn_kernel(mixer, *refs):
    if mixer in ("pool", "pool_first"):
        h_ref, halo_ref, mg_ref, pw_ref, ps_ref = refs[:5]
        refs = refs[5:]
        natural = mixer == "pool_first"
        x = _pool_rows(pl.program_id(0), h_ref[...] if natural else _load_h(h_ref),
                       halo_ref[...] if natural else _load_h(halo_ref), mg_ref[...], pw_ref, ps_ref)
    else:
        a_ref, w_ref, h_ref = refs[:3]
        refs = refs[3:]
        y = _dot(a_ref[...], w_ref[...])
        if mixer == "glu":
            y = y[:, :D_MODEL] * (1.0 / (1.0 + jnp.exp(-y[:, D_MODEL:])))
        x = _load_h(h_ref) + y
    g_ref, wq_ref, k_ref, v_ref, wo_ref, o_ref = refs
    hn = _rms(x, g_ref[...]).astype(BF16)
    q = (_dot(hn, wq_ref[...]) * (XA_HEAD_DIM ** -0.5)).astype(BF16)
    heads = []
    for hd in range(XA_HEADS):
        sl = slice(hd * XA_HEAD_DIM, (hd + 1) * XA_HEAD_DIM)
        s = _dot_nt(q[:, sl], k_ref[:, sl])
        e = jnp.exp(s - jnp.max(s, axis=-1, keepdims=True))
        denom = jnp.sum(e, axis=-1, keepdims=True)
        heads.append((_dot(e.astype(BF16), v_ref[:, sl]) / denom).astype(BF16))
    o = jnp.concatenate(heads, axis=1)
    _store_h(o_ref, x + _dot(o, wo_ref[...]))


def _mixer_xattn_layer(mixer, mixer_args, g, wq, kv, wo, layer):
    tiles_per_seq = SEQ // ROW_TILE
    row = lambda i: (0, 0)
    if mixer in ("pool", "pool_first"):
        h, mg, (pw, pj), ps = mixer_args
        before = lambda i: jnp.maximum(i * (ROW_TILE // POOL_HALO) - 1, 0)
        if mixer == "pool":
            h_specs = [_h_spec(ROW_TILE), _h_spec(POOL_HALO, lambda i: (0, before(i), 0))]
        else:
            h_specs = [pl.BlockSpec((ROW_TILE, D_MODEL), lambda i: (i, 0)),
                       pl.BlockSpec((POOL_HALO, D_MODEL), lambda i: (before(i), 0))]
        args = (h, h, mg.reshape(1, D_MODEL), pw, ps.reshape(1, D_MODEL))
        specs = h_specs + [pl.BlockSpec((1, D_MODEL), row), _layer_spec(pw, pj),
                           pl.BlockSpec((1, D_MODEL), row)]
    else:
        a, (w, wj), h = mixer_args
        args = (a, w, h)
        specs = [pl.BlockSpec((ROW_TILE, a.shape[1]), lambda i: (i, 0)),
                 _layer_spec(w, wj), _h_spec(ROW_TILE)]
    return pl.pallas_call(
        functools.partial(_xattn_kernel, mixer),
        grid=(TOKENS // ROW_TILE,),
        in_specs=specs + [
                  pl.BlockSpec((1, D_MODEL), row),
                  _layer_spec(wq, layer),
                  pl.BlockSpec((None, MEM_LEN, D_MODEL), lambda i: (layer, i // tiles_per_seq, 0)),
                  pl.BlockSpec((None, MEM_LEN, D_MODEL), lambda i: (layer, i // tiles_per_seq, 1)),
                  _layer_spec(wo, layer)],
        out_specs=_h_spec(ROW_TILE),
        out_shape=H_SHAPE,
        compiler_params=_params("parallel"),
        name="mixer_xattn",
    )(*args, g.reshape(1, D_MODEL), wq, kv, kv, wo)


def _ffn_kernel(with_next_norm, h_ref, halo_ref, g_ref, wup_ref, cw_ref, cb_ref, wdn_ref, *rest):
    if with_next_norm:
        next_g_ref, o_ref, next_ref, act_ref = rest
    else:
        o_ref, act_ref = rest
    i = pl.program_id(0)
    g = g_ref[...]
    seq_start = (i % (SEQ // ROW_TILE)) == 0
    residue = lambda ref, c, s: ref[c, pl.ds(s, FFN_GROUPS, stride=FFN_RES), :]
    xs = [jnp.concatenate([residue(h_ref, c, s) for c in range(PANELS)], axis=1)
          for s in range(FFN_RES)]
    halo = jnp.where(seq_start, 0.0, _rms(_load_h(halo_ref), g))
    hn = jnp.concatenate([halo] + [_rms(x, g) for x in xs], axis=0).astype(BF16)
    first_group = lax.broadcasted_iota(jnp.int32, (FFN_GROUPS, 1), 0) == 0

    def conv(u, cols):
        cw = cw_ref[:, cols]
        cb = cb_ref[:, cols]
        blk = lambda s: u[FFN_HALO + s * FFN_GROUPS:FFN_HALO + (s + 1) * FFN_GROUPS]
        wrapped = lambda s, halo_row: jnp.where(first_group, u[halo_row:halo_row + 1],
                                                _shift_rows(blk(s), 1))
        b = [wrapped(FFN_RES - 2, FFN_HALO - 2), wrapped(FFN_RES - 1, FFN_HALO - 1)]
        b += [blk(s) for s in range(FFN_RES)]
        return jnp.concatenate(
            [cw[2:3] * b[s + 2] + cw[1:2] * b[s + 1] + cw[0:1] * b[s] + cb for s in range(FFN_RES)],
            axis=0)

    n_chunks = D_FF // FFN_CHUNK
    cols = lambda f, base: slice(base + f * FFN_CHUNK, base + (f + 1) * FFN_CHUNK)
    up = lambda f: (_dot(hn, wup_ref[:, cols(f, 0)]), _dot(hn, wup_ref[:, cols(f, D_FF)]))
    ahead = up(0)
    for f in range(n_chunks):
        vcols, gcols = cols(f, 0), cols(f, D_FF)
        u_val, u_gate = ahead
        if f + 1 < n_chunks:
            ahead = up(f + 1)
        val = conv(u_val, vcols)
        gate = conv(u_gate, gcols)
        act_ref[:, vcols] = (gate * (1.0 / (1.0 + jnp.exp(-gate))) * val).astype(BF16)
    down = _dot(act_ref[...], wdn_ref[...])
    for s in range(FFN_RES):
        y = xs[s] + down[s * FFN_GROUPS:(s + 1) * FFN_GROUPS, :]
        outs = [(o_ref, y)] + ([(next_ref, _rms(y, next_g_ref[...]))] if with_next_norm else [])
        for ref, val in outs:
            for c in range(PANELS):
                ref[c, pl.ds(s, FFN_GROUPS, stride=FFN_RES), :] = val[:, c * LANES:(c + 1) * LANES]


def _ffn_layer(h, g, w_up, conv_w, conv_b, w_down, layer, next_g=None):
    halo_blocks = ROW_TILE // FFN_HALO
    resident = dict(pipeline_mode=pl.Buffered(1))
    with_next = next_g is not None
    gain = pl.BlockSpec((1, D_MODEL), lambda i: (0, 0))
    return pl.pallas_call(
        functools.partial(_ffn_kernel, with_next),
        grid=(TOKENS // ROW_TILE,),
        in_specs=[_h_spec(ROW_TILE),
                  _h_spec(FFN_HALO, lambda i: (0, jnp.maximum(i * halo_blocks - 1, 0), 0)),
                  pl.BlockSpec((1, D_MODEL), lambda i: (0, 0)),
                  _layer_spec(w_up, layer, **resident),
                  pl.BlockSpec((CONV_WIDTH, 2 * D_FF), lambda i: (0, 0)),
                  pl.BlockSpec((1, 2 * D_FF), lambda i: (0, 0)),
                  _layer_spec(w_down, layer, **resident)] + [gain] * with_next,
        out_specs=[_h_spec(ROW_TILE)] * 2 if with_next else _h_spec(ROW_TILE),
        out_shape=[H_SHAPE] * 2 if with_next else H_SHAPE,
        scratch_shapes=[pltpu.VMEM((ROW_TILE, D_FF), BF16)],
        compiler_params=_params("arbitrary"),
        name="conv_glu_ffn",
    )(h, h, g.reshape(1, D_MODEL), w_up, conv_w, conv_b.reshape(1, 2 * D_FF), w_down,
      *([next_g.reshape(1, D_MODEL)] if with_next else []))


def kernel(x, mem, mix_norm_g, pool_w, pool_scale, sb_w_qkv, sb_w_o, s5_a_re, s5_a_im, s5_log_dt, s5_b_re, s5_b_im, s5_c_re, s5_c_im, s5_d, s5_w_glu, xa_norm_g, mem_norm_g, xa_wq, xa_wkv, xa_wo, ffn_norm_g, ffn_w_up, ffn_conv_w, ffn_conv_b, ffn_w_down, final_norm_g):
    h = x.reshape(TOKENS, D_MODEL)
    mem2d = mem.reshape(BATCH * MEM_LEN, D_MODEL)
    pool_w, sb_w_qkv, sb_w_o, s5_w_glu, xa_wq, xa_wkv, xa_wo, ffn_w_up, ffn_w_down = (
        w.astype(BF16) for w in (pool_w, sb_w_qkv, sb_w_o, s5_w_glu, xa_wq, xa_wkv, xa_wo,
                                 ffn_w_up, ffn_w_down))
    kv = _memory_kv(mem2d, mem_norm_g, xa_wkv)
    hn = None
    for i in range(DEPTH):
        kind = i % N_MIXERS
        j = i // N_MIXERS
        if kind == 0:
            mixer = "pool" if i > 0 else "pool_first"
            mixer_args = (h, mix_norm_g[i], (pool_w, j), pool_scale[j])
        elif kind == 1:
            qkv = _norm_matmul(h, mix_norm_g[i], sb_w_qkv, j, "sb_qkv", panel_in=True)
            mixer, mixer_args = "proj", (_sb_attention(qkv), (sb_w_o, j), h)
        else:
            ops = _s5_operators(s5_a_re[j], s5_a_im[j], s5_log_dt[j], s5_b_re[j], s5_b_im[j],
                                s5_c_re[j], s5_c_im[j])
            y = _s5_mixer_core(hn, ops, s5_d[j])
            mixer, mixer_args = "glu", (y, (s5_w_glu, j), h)
        h = _mixer_xattn_layer(mixer, mixer_args, xa_norm_g[i], xa_wq, kv, xa_wo, i)
        next_is_s5 = i + 1 < DEPTH and (i + 1) % N_MIXERS == 2
        out = _ffn_layer(h, ffn_norm_g[i], ffn_w_up, ffn_conv_w[i], ffn_conv_b[i], ffn_w_down, i,
                         mix_norm_g[i + 1] if next_is_s5 else None)
        h, hn = out if next_is_s5 else (out, None)
    return _norm(h, final_norm_g, panel_out=False).reshape(BATCH, SEQ, D_MODEL)
```
